```python
import jax, jax.numpy as jnp
from jax import lax
import numpy as np

D_MODEL = 1024
BATCH = 2
SEQ = 8192
DEPTH = 1
DEC_BATCH = 16
DEC_SEQ = 4096
PAST_LEN = 128

D_MIX = D_MODEL
D_RWKV = D_MIX // 2
RWKV_HEAD = 64
RWKV_HEADS = D_RWKV // RWKV_HEAD
D_HGRN = D_MIX - D_RWKV
HGRN_HEAD = 128
HGRN_HEADS = D_HGRN // HGRN_HEAD
DECAY_LORA = 64
AAA_LORA = 64
GATE_LORA = 128
HGRN_CHUNK = 32
N_GROUPS = 4
EXPERTS_PER_GROUP = 8
N_EXPERTS = N_GROUPS * EXPERTS_PER_GROUP
TOP_K_IN_GROUP = 2
D_EXPERT = 512
MOE_BLOCK = 128
NORM_EPS = 1e-6
HGRN_NORM_EPS = 1e-5
LNX_EPS = RWKV_HEAD * 1e-5

C_R = 0
C_K = C_R + D_RWKV
C_V = C_K + D_RWKV
C_WD = C_V + D_RWKV
C_AD = C_WD + 2 * DECAY_LORA
C_GD = C_AD + 2 * AAA_LORA
C_RWKV_END = C_GD + GATE_LORA
C_Q = C_RWKV_END
C_F = C_Q + D_HGRN
C_I = C_F + 2 * D_HGRN
C_G = C_I + D_HGRN
D_IN = C_G + D_HGRN

kernel_name = "hymba_rwkv7_hgrn2_hmoe_encoder"


def _rms_norm(x, w, eps=NORM_EPS):
    xf = x.astype(jnp.float32)
    y = xf * lax.rsqrt(jnp.mean(xf * xf, axis=-1, keepdims=True) + eps)
    return (y * w.astype(jnp.float32)).astype(x.dtype)


def _centred_shift(u):
    prev = jnp.pad(u[:, :-1], ((0, 0), (1, 0), (0, 0)))
    nxt = jnp.pad(u[:, 1:], ((0, 0), (0, 1), (0, 0)))
    return 0.5 * (prev + nxt)


def _rwkv7_mixer(u, w0, w2, a0, a2, g2, k_k, k_a, r_k, lnx_w, lnx_b):
    B, T, _ = u.shape
    H, K = RWKV_HEADS, RWKV_HEAD
    f32 = jnp.float32
    r = u[..., C_R:C_K]
    k = u[..., C_K:C_V]
    v = u[..., C_V:C_WD]
    wd = u[..., C_WD:C_AD].reshape(B, T, 2, DECAY_LORA)
    ad = u[..., C_AD:C_GD].reshape(B, T, 2, AAA_LORA)
    gd = u[..., C_GD:C_RWKV_END]
    w_log = -jax.nn.softplus(-(w0[:, None, None, :] + jnp.einsum('btdr,drc->dbtc', jnp.tanh(wd), w2))) - 0.5
    decay = jnp.exp(-jnp.exp(w_log.astype(f32)))
    a = jax.nn.sigmoid(a0[:, None, None, :] + jnp.einsum('btdr,drc->dbtc', ad, a2))
    g = jnp.einsum('btr,rc->btc', jax.nn.sigmoid(gd), g2)
    kk = (k * k_k).reshape(B, T, H, K).astype(f32)
    kk = kk / jnp.maximum(jnp.linalg.norm(kk, axis=-1, keepdims=True), 1e-12)
    kk = kk.reshape(B, T, D_RWKV)
    k_dir = k[None] * (1 + (a - 1) * k_a)
    b_dir = kk[None] * a

    def to_scan(z):
        z = jnp.stack([z[0], jnp.flip(z[1], axis=1)]).astype(f32)
        return jnp.moveaxis(z.reshape(2, B, T, H, K), 2, 0)

    def both(z):
        return to_scan(jnp.broadcast_to(z, (2,) + z.shape))

    xs = (both(r), to_scan(decay), to_scan(k_dir), both(v), both(kk), to_scan(b_dir))

    def step(S, inp):
        r_t, w_t, k_t, v_t, kk_t, b_t = inp
        sa = jnp.einsum('dbhvk,dbhk->dbhv', S, -kk_t)
        S = S * w_t[..., None, :] + sa[..., None] * b_t[..., None, :] + v_t[..., None] * k_t[..., None, :]
        y_t = jnp.einsum('dbhvk,dbhk->dbhv', S, r_t)
        return S, y_t

    S0 = jnp.zeros((2, B, H, K, K), f32)
    _, y = lax.scan(step, S0, xs)
    y = jnp.moveaxis(y, 0, 2)
    y = y[0] + jnp.flip(y[1], axis=1)
    mean = jnp.mean(y, axis=-1, keepdims=True)
    var = jnp.mean(jnp.square(y - mean), axis=-1, keepdims=True)
    y = ((y - mean) * lax.rsqrt(var + LNX_EPS)).reshape(B, T, D_RWKV)
    y = y * lnx_w.astype(f32) + lnx_b.astype(f32)
    bonus = jnp.einsum('bthk,dbthk,hk->bth', r.reshape(B, T, H, K).astype(f32),
                       k_dir.reshape(2, B, T, H, K).astype(f32), r_k.astype(f32))
    y = y + (bonus[..., None] * v.reshape(B, T, H, K).astype(f32)).reshape(B, T, D_RWKV)
    return (y * g.astype(f32)).astype(u.dtype)


def _hgrn2_mixer(u, lb, gnorm_w):
    B, T, _ = u.shape
    H, K, C = HGRN_HEADS, HGRN_HEAD, HGRN_CHUNK
    NC = T // C
    f32 = jnp.float32
    q = jax.nn.silu(u[..., C_Q:C_F].astype(f32))
    f_raw = jnp.moveaxis(u[..., C_F:C_I].reshape(B, T, 2, D_HGRN).astype(f32), 2, 0)
    i_in = u[..., C_I:C_G].astype(f32)
    gate = u[..., C_G:D_IN]
    lbb = lb[:, None, None, :].astype(f32)
    f = lbb + (1 - lbb) * jax.nn.sigmoid(f_raw)
    log_f = jnp.log(f)
    k = 1 - f

    def to_chunks(z):
        z = jnp.stack([z[0], jnp.flip(z[1], axis=1)])
        z = z.reshape(2, B, NC, C, H, K)
        return jnp.transpose(z, (2, 0, 1, 4, 3, 5))

    def both(z):
        return to_chunks(jnp.broadcast_to(z, (2,) + z.shape))

    mask = jnp.tril(jnp.ones((C, C), dtype=bool))

    def step(S, inp):
        q_c, k_c, v_c, g_c = inp
        G = jnp.cumsum(g_c, axis=-2)
        rel = jnp.where(mask[:, :, None],
                        jnp.exp(jnp.minimum(G[..., :, None, :] - G[..., None, :, :], 0.0)), 0.0)
        A = jnp.einsum('...tk,...sk,...tsk->...ts', q_c, k_c, rel)
        o = jnp.einsum('...ts,...sv->...tv', A, v_c) + jnp.einsum('...tk,...kv->...tv', q_c * jnp.exp(G), S)
        G_last = G[..., -1:, :]
        S = jnp.exp(G_last[..., 0, :])[..., None] * S + \
            jnp.einsum('...sk,...sv->...kv', k_c * jnp.exp(G_last - G), v_c)
        return S, o

    S0 = jnp.zeros((2, B, H, K, K), f32)
    _, o = lax.scan(step, S0, (both(q), to_chunks(k), both(i_in), to_chunks(log_f)))
    o = jnp.transpose(o, (1, 2, 0, 4, 3, 5)).reshape(2, B, T, H, K)
    o = o[0] + jnp.flip(o[1], axis=1)
    o = o * lax.rsqrt(jnp.mean(o * o, axis=-1, keepdims=True) + HGRN_NORM_EPS) * gnorm_w.astype(f32)
    o = o.reshape(B, T, D_HGRN) * jax.nn.silu(gate.astype(f32))
    return o.astype(u.dtype)


def _hier_moe(h, wg, bg, we, be, w_gate, w_up, w_down):
    N, D = h.shape
    glog = (h @ wg + bg).astype(jnp.float32)
    gprob = jax.nn.softmax(glog, axis=-1)
    grp = jnp.argmax(glog, axis=-1)
    p_grp = jnp.take_along_axis(gprob, grp[:, None], axis=-1)
    elog = (h @ we + be).astype(jnp.float32).reshape(N, N_GROUPS, EXPERTS_PER_GROUP)
    elog_g = jnp.take_along_axis(elog, grp[:, None, None], axis=1)[:, 0]
    top_v, top_i = lax.top_k(elog_g, TOP_K_IN_GROUP)
    gate = p_grp * jax.nn.softmax(top_v, axis=-1)
    eid = grp[:, None] * EXPERTS_PER_GROUP + top_i

    M = N * TOP_K_IN_GROUP
    eid_f = eid.reshape(M)
    tok = jnp.repeat(jnp.arange(N), TOP_K_IN_GROUP)
    gate_f = gate.reshape(M)
    order = jnp.argsort(eid_f)
    e_sorted = eid_f[order]
    tok_sorted = tok[order]
    gate_sorted = gate_f[order]
    counts = jnp.bincount(eid_f, length=N_EXPERTS)
    starts = jnp.cumsum(counts) - counts
    padded = (counts + MOE_BLOCK - 1) // MOE_BLOCK * MOE_BLOCK
    pad_ends = jnp.cumsum(padded)
    pad_starts = pad_ends - padded
    dest = pad_starts[e_sorted] + (jnp.arange(M) - starts[e_sorted])
    n_blocks = (M + N_EXPERTS * (MOE_BLOCK - 1) + MOE_BLOCK - 1) // MOE_BLOCK
    P = n_blocks * MOE_BLOCK
    xbuf = jnp.zeros((P, D), h.dtype).at[dest].set(h[tok_sorted])
    block_e = jnp.minimum(jnp.sum(pad_ends[None, :] <= (jnp.arange(n_blocks) * MOE_BLOCK)[:, None], axis=1),
                          N_EXPERTS - 1)

    def run_block(args):
        xb, e = args
        hid = jax.nn.silu(xb @ w_gate[e]) * (xb @ w_up[e])
        return hid @ w_down[e]

    ybuf = lax.map(run_block, (xbuf.reshape(n_blocks, MOE_BLOCK, D), block_e)).reshape(P, D)
    contrib = ybuf[dest] * gate_sorted[:, None].astype(h.dtype)
    return jax.ops.segment_sum(contrib, tok_sorted, num_segments=N)


def _forward(x, norm1_w, w_in, mu_shift, rwkv_w0, rwkv_w2, rwkv_a0, rwkv_a2, rwkv_g2, rwkv_k_k, rwkv_k_a,
             rwkv_r_k, rwkv_lnx_w, rwkv_lnx_b, hgrn_lb, hgrn_gnorm_w, w_out, norm2_w, router_group_w,
             router_group_b, router_expert_w, router_expert_b, moe_w_gate, moe_w_up, moe_w_down, final_norm_w):
    B, T, D = x.shape
    lb_all = jnp.cumsum(jax.nn.softmax(hgrn_lb.astype(jnp.float32), axis=1), axis=1)
    for i in range(DEPTH):
        h = _rms_norm(x, norm1_w[i])
        u = jnp.einsum('btd,dc->btc', h, w_in[i])
        u_r = u[..., :C_RWKV_END]
        u_r = u_r + (_centred_shift(u_r) - u_r) * mu_shift[i]
        y_a = _rwkv7_mixer(u_r, rwkv_w0[i], rwkv_w2[i], rwkv_a0[i], rwkv_a2[i], rwkv_g2[i], rwkv_k_k[i],
                           rwkv_k_a[i], rwkv_r_k[i], rwkv_lnx_w[i], rwkv_lnx_b[i])
        y_b = _hgrn2_mixer(u, lb_all[:, i], hgrn_gnorm_w[i])
        x = x + jnp.einsum('btc,cd->btd', jnp.concatenate([y_a, y_b], axis=-1), w_out[i])
        h2 = _rms_norm(x, norm2_w[i]).reshape(B * T, D)
        x = x + _hier_moe(h2, router_group_w[i], router_group_b[i], router_expert_w[i], router_expert_b[i],
                          moe_w_gate[i], moe_w_up[i], moe_w_down[i]).reshape(B, T, D)
    return _rms_norm(x, final_norm_w)


def setup_inputs(seed: int = 0) -> dict:
    key = jax.random.key(seed)
    ks = jax.random.split(key, 32)
    f32 = jnp.float32
    nrm = lambda k, s, sc: jax.random.normal(k, s, f32) * sc
    L = DEPTH
    return {
        "x_prompt": nrm(ks[0], (BATCH, SEQ, D_MODEL), 1.0),
        "x_sample": nrm(ks[1], (DEC_BATCH, DEC_SEQ, D_MODEL), 1.0),
        "norm1_w": 1.0 + nrm(ks[2], (L, D_MODEL), 0.02),
        "w_in": nrm(ks[3], (L, D_MODEL, D_IN), D_MODEL ** -0.5),
        "mu_shift": jax.random.uniform(ks[4], (L, C_RWKV_END), f32),
        "rwkv_w0": jax.random.uniform(ks[5], (L, 2, D_RWKV), f32, -6.0, -1.0),
        "rwkv_w2": nrm(ks[6], (L, 2, DECAY_LORA, D_RWKV), 0.1),
        "rwkv_a0": nrm(ks[7], (L, 2, D_RWKV), 0.1),
        "rwkv_a2": nrm(ks[8], (L, 2, AAA_LORA, D_RWKV), 0.5 * AAA_LORA ** -0.5),
        "rwkv_g2": nrm(ks[9], (L, GATE_LORA, D_RWKV), GATE_LORA ** -0.5),
        "rwkv_k_k": 0.85 + nrm(ks[10], (L, D_RWKV), 0.02),
        "rwkv_k_a": 1.0 + nrm(ks[11], (L, D_RWKV), 0.02),
        "rwkv_r_k": nrm(ks[12], (L, RWKV_HEADS, RWKV_HEAD), 0.1),
        "rwkv_lnx_w": 1.0 + nrm(ks[13], (L, D_RWKV), 0.02),
        "rwkv_lnx_b": nrm(ks[14], (L, D_RWKV), 0.01),
        "hgrn_lb": nrm(ks[15], (2, L + 1, D_HGRN), 0.5),
        "hgrn_gnorm_w": 1.0 + nrm(ks[16], (L, HGRN_HEAD), 0.02),
        "w_out": nrm(ks[17], (L, D_MIX, D_MODEL), D_MIX ** -0.5),
        "norm2_w": 1.0 + nrm(ks[18], (L, D_MODEL), 0.02),
        "router_group_w": nrm(ks[19], (L, D_MODEL, N_GROUPS), D_MODEL ** -0.5),
        "router_group_b": nrm(ks[20], (L, N_GROUPS), 0.01),
        "router_expert_w": nrm(ks[21], (L, D_MODEL, N_EXPERTS), D_MODEL ** -0.5),
        "router_expert_b": nrm(ks[22], (L, N_EXPERTS), 0.01),
        "moe_w_gate": nrm(ks[23], (L, N_EXPERTS, D_MODEL, D_EXPERT), D_MODEL ** -0.5),
        "moe_w_up": nrm(ks[24], (L, N_EXPERTS, D_MODEL, D_EXPERT), D_MODEL ** -0.5),
        "moe_w_down": nrm(ks[25], (L, N_EXPERTS, D_EXPERT, D_MODEL), D_EXPERT ** -0.5),
        "final_norm_w": 1.0 + nrm(ks[26], (D_MODEL,), 0.02),
    }


def reference(x_prompt, x_sample, norm1_w, w_in, mu_shift, rwkv_w0, rwkv_w2, rwkv_a0, rwkv_a2, rwkv_g2,
              rwkv_k_k, rwkv_k_a, rwkv_r_k, rwkv_lnx_w, rwkv_lnx_b, hgrn_lb, hgrn_gnorm_w, w_out, norm2_w,
              router_group_w, router_group_b, router_expert_w, router_expert_b, moe_w_gate, moe_w_up,
              moe_w_down, final_norm_w):
    weights = (norm1_w, w_in, mu_shift, rwkv_w0, rwkv_w2, rwkv_a0, rwkv_a2, rwkv_g2, rwkv_k_k, rwkv_k_a,
               rwkv_r_k, rwkv_lnx_w, rwkv_lnx_b, hgrn_lb, hgrn_gnorm_w, w_out, norm2_w, router_group_w,
               router_group_b, router_expert_w, router_expert_b, moe_w_gate, moe_w_up, moe_w_down, final_norm_w)
    y_prompt = _forward(x_prompt, *weights)
    y_sample = _forward(x_sample, *weights)
    return (y_prompt, y_sample)
```

```python
import functools

import jax
import jax.numpy as jnp
from jax import lax
from jax.experimental import pallas as pl
from jax.experimental.pallas import tpu as pltpu

F32 = jnp.float32
BF16 = jnp.bfloat16

D_MODEL = 1024
D_RWKV = 512
RWKV_HEAD = 64
RWKV_HEADS = 8
D_HGRN = 512
HGRN_HEAD = 128
HGRN_HEADS = 4
DECAY_LORA = 64
AAA_LORA = 64
GATE_LORA = 128
N_GROUPS = 4
EXPERTS_PER_GROUP = 8
N_EXPERTS = 32
D_EXPERT = 512
NORM_EPS = 1e-6
HGRN_NORM_EPS = 1e-5
LNX_EPS = RWKV_HEAD * 1e-5

C_K = 512
C_V = 1024
C_WD = 1536
C_AD = 1664
C_GD = 1792
C_RWKV_END = 1920
D_IN = 4480
HG_Q = 0
HG_F = 512
HG_I = 1536
HG_G = 2048
D_HG_IN = 2560

SUBLANES = 8
LANES = 128
VMEM_LIMIT = 56 * 1024 * 1024

PROJ_ROWS = 256
HALO = 8
SCAN_STEPS = 32
HGRN_CHUNK = 32
HGRN_TILE = 256
MOE_ROWS = 256
ROUTER_PAD = 128


def _dot(a, b):
    return jnp.dot(a, b, preferred_element_type=F32)


def _dot_nt(a, b):
    return lax.dot_general(a, b, (((1,), (1,)), ((), ())), preferred_element_type=F32)


def _dot_tn(a, b):
    return lax.dot_general(a, b, (((0,), (0,)), ((), ())), preferred_element_type=F32)


def _split2(a):
    hi = a.astype(BF16)
    lo = (a - hi.astype(F32)).astype(BF16)
    return hi, lo


def _split3(a):
    hi = a.astype(BF16)
    r1 = a - hi.astype(F32)
    mid = r1.astype(BF16)
    lo = (r1 - mid.astype(F32)).astype(BF16)
    return hi, mid, lo


def _dot_lhs2(a, b_bf16):
    hi, lo = _split2(a)
    return _dot(hi, b_bf16) + _dot(lo, b_bf16)


def _sigmoid(x):
    return 1.0 / (1.0 + jnp.exp(-x))


def _softplus(x):
    return jnp.maximum(x, 0.0) + jnp.log(1.0 + jnp.exp(-jnp.abs(x)))


def _proj_kernel(x_ref, xp_ref, xn_ref, n1_ref, wr_ref, wh_ref, mu_ref, w0_ref, w2_ref, a0_ref, a2_ref,
                 g2_ref, kk_ref, ka_ref, rk_ref, lb_ref, hd_ref,
                 r_o, v_o, kn_o, dec_o, kd_o, bd_o, g_o, bv_o, q_o, lf_o, kf_o, iv_o, sg_o,
                 u_scr):
    i = pl.program_id(1)
    n_i = pl.num_programs(1)
    rows = x_ref.shape[1]

    xe = jnp.concatenate([xp_ref[0] * (i > 0).astype(F32), x_ref[0],
                          xn_ref[0] * (i < n_i - 1).astype(F32)], axis=0)
    ms = jnp.mean(xe * xe, axis=-1, keepdims=True)
    h_e = (xe * lax.rsqrt(ms + NORM_EPS) * n1_ref[...]).astype(BF16)

    u_scr[...] = _dot(h_e, wr_ref[...])
    u_c = u_scr[pl.ds(HALO, rows), :]
    u_prev = u_scr[pl.ds(HALO - 1, rows), :]
    u_next = u_scr[pl.ds(HALO + 1, rows), :]
    us = u_c + (0.5 * (u_prev + u_next) - u_c) * mu_ref[...]

    r = us[:, 0:C_K]
    k = us[:, C_K:C_V]
    v = us[:, C_V:C_WD]
    hd = hd_ref[...]
    kk = k * kk_ref[...]
    ss = _dot_lhs2(kk * kk, hd)
    kn = kk / jnp.maximum(jnp.sqrt(ss), 1e-12)
    gd = us[:, C_GD:C_RWKV_END]
    g_o[0] = _dot(_sigmoid(gd).astype(BF16), g2_ref[...])
    r_o[0] = r
    v_o[0] = v
    kn_o[0] = kn
    kd_sum = jnp.zeros_like(k)
    for d in range(2):
        wd = us[:, C_WD + d * DECAY_LORA:C_WD + (d + 1) * DECAY_LORA]
        ad = us[:, C_AD + d * AAA_LORA:C_AD + (d + 1) * AAA_LORA]
        zw = w0_ref[d:d + 1, :] + _dot(jnp.tanh(wd).astype(BF16), w2_ref[d])
        w_log = -_softplus(-zw) - 0.5
        dec_o[d, 0] = jnp.exp(-jnp.exp(w_log))
        a = _sigmoid(a0_ref[d:d + 1, :] + _dot(ad.astype(BF16), a2_ref[d]))
        kd = k * (1.0 + (a - 1.0) * ka_ref[...])
        kd_o[d, 0] = kd
        bd_o[d, 0] = kn * a
        kd_sum = kd_sum + kd
    bonus = _dot_lhs2(r * kd_sum * rk_ref[...], hd)
    bv_o[0] = bonus * v

    uh = _dot(h_e, wh_ref[...])[HALO:HALO + rows]
    uq = uh[:, HG_Q:HG_F]
    q_o[0] = uq * _sigmoid(uq)
    iv_o[0] = uh[:, HG_I:HG_G]
    ug = uh[:, HG_G:D_HG_IN]
    sg_o[0] = ug * _sigmoid(ug)
    for d in range(2):
        fr = uh[:, HG_F + d * D_HGRN:HG_F + (d + 1) * D_HGRN]
        lb = lb_ref[d:d + 1, :]
        f = lb + (1.0 - lb) * _sigmoid(fr)
        lf_o[d, 0] = jnp.log(f)
        kf_o[d, 0] = 1.0 - f


def _proj_call(x, p):
    B, T, D = x.shape
    rows = min(PROJ_ROWS, T)
    n_i = T // rows
    rb = rows // HALO

    def full(a):
        nd = a.ndim
        return pl.BlockSpec(a.shape, lambda b, i, _n=nd: (0,) * _n)

    tile = pl.BlockSpec((1, rows, D), lambda b, i: (b, i, 0))
    prev = pl.BlockSpec((1, HALO, D), lambda b, i: (b, jnp.maximum(i * rb - 1, 0), 0))
    nxt = pl.BlockSpec((1, HALO, D), lambda b, i: (b, jnp.minimum((i + 1) * rb, T // HALO - 1), 0))
    consts = [p["norm1_w"], p["w_r"], p["w_h"], p["mu"], p["w0"], p["w2"], p["a0"], p["a2"], p["g2"],
              p["k_k"], p["k_a"], p["r_k"], p["lb"], p["hd64"]]
    one = jax.ShapeDtypeStruct((B, T, D_RWKV), F32)
    two = jax.ShapeDtypeStruct((2, B, T, D_RWKV), F32)
    one_spec = pl.BlockSpec((1, rows, D_RWKV), lambda b, i: (b, i, 0))
    two_spec = pl.BlockSpec((2, 1, rows, D_RWKV), lambda b, i: (0, b, i, 0))
    kinds = [one, one, one, two, two, two, one, one, one, two, two, one, one]
    return pl.pallas_call(
        _proj_kernel,
        out_shape=kinds,
        grid=(B, n_i),
        in_specs=[tile, prev, nxt] + [full(a) for a in consts],
        out_specs=[one_spec if s is one else two_spec for s in kinds],
        scratch_shapes=[pltpu.VMEM((rows + 2 * HALO, C_RWKV_END), F32)],
        compiler_params=pltpu.CompilerParams(
            dimension_semantics=("parallel", "arbitrary"), vmem_limit_bytes=VMEM_LIMIT),
        name="proj",
    )(x, x, x, *consts)


def _rwkv_kernel(r_ref, kk_ref, v_ref, w_ref, k_ref, b_ref, y_ref, s_ref, *, nv, steps):
    d = pl.program_id(0)
    i = pl.program_id(1)
    K = RWKV_HEAD
    n_par = max(1, SUBLANES // nv)

    @pl.when(i == 0)
    def _():
        s_ref[...] = jnp.zeros_like(s_ref)

    def row(ref, t, k):
        return jnp.broadcast_to(ref[t, pl.ds(k, 1), :], (SUBLANES, LANES))

    def tree_sum(parts):
        parts = [x for x in parts if x is not None]
        while len(parts) > 1:
            parts = [parts[j] + parts[j + 1] if j + 1 < len(parts) else parts[j]
                     for j in range(0, len(parts), 2)]
        return parts[0]

    def step(j, carry):
        t = jnp.where(d == 0, j, steps - 1 - j)
        acc = [[None] * n_par for _ in range(nv)]
        for k in range(K):
            kk_row = row(kk_ref, t, k)
            for vg in range(nv):
                term = s_ref[vg * K + k] * kk_row
                slot = k % n_par
                acc[vg][slot] = term if acc[vg][slot] is None else acc[vg][slot] + term
        sa = [-tree_sum(acc[vg]) for vg in range(nv)]
        val = [v_ref[t, pl.ds(vg * SUBLANES, SUBLANES), :] for vg in range(nv)]
        yacc = [[None] * n_par for _ in range(nv)]
        for k in range(K):
            w_row = row(w_ref, t, k)
            b_row = row(b_ref, t, k)
            k_row = row(k_ref, t, k)
            r_row = row(r_ref, t, k)
            for vg in range(nv):
                s_new = s_ref[vg * K + k] * w_row + sa[vg] * b_row + val[vg] * k_row
                s_ref[vg * K + k] = s_new
                term = s_new * r_row
                slot = k % n_par
                yacc[vg][slot] = term if yacc[vg][slot] is None else yacc[vg][slot] + term
        for vg in range(nv):
            y_ref[t, pl.ds(vg * SUBLANES, SUBLANES), :] = tree_sum(yacc[vg])
        return carry

    lax.fori_loop(0, steps, step, 0)


def _rwkv_call(r, kk, v, w, k, b, nv):
    T = r.shape[0]
    steps = min(SCAN_STEPS, T)
    n_t = T // steps

    def tblk(d, i):
        return jnp.where(d == 0, i, n_t - 1 - i)

    shared = pl.BlockSpec((steps, RWKV_HEAD, LANES), lambda d, i: (tblk(d, i), 0, 0))
    vspec = pl.BlockSpec((steps, nv * SUBLANES, LANES), lambda d, i: (tblk(d, i), 0, 0))
    perdir = pl.BlockSpec((None, steps, RWKV_HEAD, LANES), lambda d, i: (d, tblk(d, i), 0, 0))
    yspec = pl.BlockSpec((None, steps, nv * SUBLANES, LANES), lambda d, i: (d, tblk(d, i), 0, 0))
    return pl.pallas_call(
        functools.partial(_rwkv_kernel, nv=nv, steps=steps),
        out_shape=jax.ShapeDtypeStruct((2, T, nv * SUBLANES, LANES), F32),
        grid=(2, n_t),
        in_specs=[shared, shared, vspec, perdir, perdir, perdir],
        out_specs=yspec,
        scratch_shapes=[pltpu.VMEM((nv * RWKV_HEAD, SUBLANES, LANES), F32)],
        compiler_params=pltpu.CompilerParams(
            dimension_semantics=("parallel", "arbitrary"), vmem_limit_bytes=VMEM_LIMIT),
        name="rwkv",
    )(r, kk, v, w, k, b)


def _to_chain_k(a, vq):
    B, T, _ = a.shape
    a = a.reshape(B, T, RWKV_HEADS, RWKV_HEAD)
    a = jnp.transpose(a, (1, 3, 0, 2)).reshape(T, RWKV_HEAD, B * RWKV_HEADS)
    return jnp.tile(a, (1, 1, vq)) if vq > 1 else a


def _to_chain_v(a, vq, nv):
    B, T, _ = a.shape
    a = a.reshape(B, T, RWKV_HEADS, vq, nv * SUBLANES)
    a = jnp.transpose(a, (1, 4, 3, 0, 2))
    return a.reshape(T, nv * SUBLANES, LANES)


def _from_chain_v(y, B, vq, nv):
    T = y.shape[0]
    y = y.reshape(T, nv * SUBLANES, vq, B, RWKV_HEADS)
    y = jnp.transpose(y, (3, 0, 4, 2, 1))
    return y.reshape(B, T, D_RWKV)


def _hgrn_direction(q, kf, val, lf, st_ref, a_scr, rev):
    C = q.shape[0]
    rowi = lax.broadcasted_iota(jnp.int32, (C, C), 0)
    coli = lax.broadcasted_iota(jnp.int32, (C, C), 1)
    seen = (coli >= rowi) if rev else (coli <= rowi)
    tri = seen.astype(BF16)
    l_hi, l_mid, l_lo = _split3(lf)
    G = _dot(tri, l_hi) + _dot(tri, l_mid) + _dot(tri, l_lo)
    last = 0 if rev else C - 1
    g_tot = G[last:last + 1, :]
    ones_row = jnp.ones((SUBLANES, HGRN_HEAD), BF16)
    outs = []
    for h in range(HGRN_HEADS):
        sl = slice(h * HGRN_HEAD, (h + 1) * HGRN_HEAD)
        Gh, qh, kh, vh = G[:, sl], q[:, sl], kf[:, sl], val[:, sl]
        st = st_ref[h]
        o_inter = _dot_nt((qh * jnp.exp(Gh)).astype(BF16), st.astype(BF16))
        for t in range(C):
            lo = (t // SUBLANES) * SUBLANES if rev else 0
            hi = C if rev else (t // SUBLANES + 1) * SUBLANES
            rel = jnp.exp(jnp.minimum(Gh[t:t + 1, :] - Gh[lo:hi, :], 0.0))
            pw = (rel * kh[lo:hi, :] * qh[t:t + 1, :]).astype(BF16)
            a_scr[t:t + 1, lo:hi] = _dot_nt(ones_row, pw)[0:1, :]
        A = jnp.where(seen, a_scr[...], 0.0)
        outs.append(o_inter + _dot(A.astype(BF16), vh.astype(BF16)))
        kdec = kh * jnp.exp(g_tot[:, sl] - Gh)
        st_ref[h] = st * jnp.exp(g_tot[:, sl]) + _dot_tn(vh.astype(BF16), kdec.astype(BF16))
    return jnp.concatenate(outs, axis=-1)


def _hgrn_kernel(qf_ref, vf_ref, lff_ref, kff_ref, qb_ref, vb_ref, lfb_ref, kfb_ref,
                 of_ref, ob_ref, stf_ref, stb_ref, a_scr, *, chunk):
    i = pl.program_id(1)
    n_c = qf_ref.shape[1] // chunk

    @pl.when(i == 0)
    def _():
        stf_ref[...] = jnp.zeros_like(stf_ref)
        stb_ref[...] = jnp.zeros_like(stb_ref)

    def body(c, carry):
        cf = pl.multiple_of(c * chunk, chunk)
        cb = pl.multiple_of((n_c - 1 - c) * chunk, chunk)
        of_ref[0, pl.ds(cf, chunk), :] = _hgrn_direction(
            qf_ref[0, pl.ds(cf, chunk), :], kff_ref[0, pl.ds(cf, chunk), :], vf_ref[0, pl.ds(cf, chunk), :],
            lff_ref[0, pl.ds(cf, chunk), :], stf_ref, a_scr, rev=False)
        ob_ref[0, pl.ds(cb, chunk), :] = _hgrn_direction(
            qb_ref[0, pl.ds(cb, chunk), :], kfb_ref[0, pl.ds(cb, chunk), :], vb_ref[0, pl.ds(cb, chunk), :],
            lfb_ref[0, pl.ds(cb, chunk), :], stb_ref, a_scr, rev=True)
        return carry

    lax.fori_loop(0, n_c, body, 0)


def _hgrn_call(q, iv, lf, kf):
    B, T, _ = q.shape
    tile = min(HGRN_TILE, T)
    chunk = min(HGRN_CHUNK, tile)
    n_i = T // tile
    fwd = pl.BlockSpec((1, tile, D_HGRN), lambda b, i: (b, i, 0))
    bwd = pl.BlockSpec((1, tile, D_HGRN), lambda b, i: (b, n_i - 1 - i, 0))
    fwd2 = pl.BlockSpec((None, 1, tile, D_HGRN), lambda b, i: (0, b, i, 0))
    bwd2 = pl.BlockSpec((None, 1, tile, D_HGRN), lambda b, i: (1, b, n_i - 1 - i, 0))
    out = jax.ShapeDtypeStruct((B, T, D_HGRN), F32)
    state = pltpu.VMEM((HGRN_HEADS, HGRN_HEAD, HGRN_HEAD), F32)
    return pl.pallas_call(
        functools.partial(_hgrn_kernel, chunk=chunk),
        out_shape=[out, out],
        grid=(B, n_i),
        in_specs=[fwd, fwd, fwd2, fwd2, bwd, bwd, bwd2, bwd2],
        out_specs=[fwd, bwd],
        scratch_shapes=[state, state, pltpu.VMEM((chunk, chunk), F32)],
        compiler_params=pltpu.CompilerParams(
            dimension_semantics=("parallel", "arbitrary"), vmem_limit_bytes=VMEM_LIMIT),
        name="hgrn",
    )(q, iv, lf, kf, q, iv, lf, kf)


def _mix_kernel(x_ref, ya_ref, g_ref, bv_ref, of_ref, ob_ref, sg_ref, lnw_ref, lnb_ref, gnw_ref,
                hd64_ref, hd128_ref, wo_ref, n2_ref, rwh_ref, rwl_ref, rb_ref,
                x1_o, h2_o, lg_o):
    hd64 = hd64_ref[...]
    y = ya_ref[0]
    inv = 1.0 / RWKV_HEAD
    mean = _dot_lhs2(y, hd64) * inv
    yc = y - mean
    var = _dot_lhs2(yc * yc, hd64) * inv
    ya = (yc * lax.rsqrt(var + LNX_EPS)) * lnw_ref[...] + lnb_ref[...]
    ya = (ya + bv_ref[0]) * g_ref[0]
    o = of_ref[0] + ob_ref[0]
    ms = _dot_lhs2(o * o, hd128_ref[...]) * (1.0 / HGRN_HEAD)
    yb = o * lax.rsqrt(ms + HGRN_NORM_EPS) * gnw_ref[...] * sg_ref[0]
    mixed = jnp.concatenate([ya, yb], axis=-1).astype(BF16)
    x1 = x_ref[0] + _dot(mixed, wo_ref[...])
    x1_o[0] = x1
    ms2 = jnp.mean(x1 * x1, axis=-1, keepdims=True)
    h2 = x1 * lax.rsqrt(ms2 + NORM_EPS) * n2_ref[...]
    h2_o[0] = h2.astype(BF16)
    h_hi, h_lo = _split2(h2)
    w_hi = rwh_ref[...]
    lg_o[0] = _dot(h_hi, w_hi) + _dot(h_lo, w_hi) + _dot(h_hi, rwl_ref[...]) + rb_ref[...]


def _mix_call(x, ya, g, bv, of, ob, sg, p):
    B, T, D = x.shape
    rows = min(PROJ_ROWS, T)

    def full(a):
        nd = a.ndim
        return pl.BlockSpec(a.shape, lambda b, i, _n=nd: (0,) * _n)

    def tile(w):
        return pl.BlockSpec((1, rows, w), lambda b, i: (b, i, 0))

    consts = [p["lnx_w"], p["lnx_b"], p["gnorm_w"], p["hd64"], p["hd128"], p["w_out"], p["norm2_w"],
              p["rw_hi"], p["rw_lo"], p["rb"]]
    return pl.pallas_call(
        _mix_kernel,
        out_shape=[jax.ShapeDtypeStruct((B, T, D), F32), jax.ShapeDtypeStruct((B, T, D), BF16),
                   jax.ShapeDtypeStruct((B, T, ROUTER_PAD), F32)],
        grid=(B, T // rows),
        in_specs=[tile(D)] + [tile(D_RWKV)] * 6 + [full(a) for a in consts],
        out_specs=[tile(D), tile(D), tile(ROUTER_PAD)],
        compiler_params=pltpu.CompilerParams(
            dimension_semantics=("parallel", "parallel"), vmem_limit_bytes=VMEM_LIMIT),
        name="mix",
    )(x, ya, g, bv, of, ob, sg, *consts)


def _moe_kernel(be_ref, x_ref, gt_ref, wg_ref, wu_ref, wd_ref, o_ref):
    del be_ref
    x = x_ref[...]
    a = _dot(x, wg_ref[0])
    u = _dot(x, wu_ref[0])
    hid = (a * _sigmoid(a)) * u
    o_ref[...] = _dot(hid.astype(BF16), wd_ref[0]) * gt_ref[...]


def _moe_call(block_e, xbuf, gate_slot, p):
    P, D = xbuf.shape
    n_blocks = P // MOE_ROWS
    grid_spec = pltpu.PrefetchScalarGridSpec(
        num_scalar_prefetch=1,
        grid=(n_blocks,),
        in_specs=[
            pl.BlockSpec((MOE_ROWS, D), lambda i, be: (i, 0)),
            pl.BlockSpec((MOE_ROWS, 1), lambda i, be: (i, 0)),
            pl.BlockSpec((1, D, D_EXPERT), lambda i, be: (be[i], 0, 0)),
            pl.BlockSpec((1, D, D_EXPERT), lambda i, be: (be[i], 0, 0)),
            pl.BlockSpec((1, D_EXPERT, D), lambda i, be: (be[i], 0, 0)),
        ],
        out_specs=pl.BlockSpec((MOE_ROWS, D), lambda i, be: (i, 0)),
    )
    return pl.pallas_call(
        _moe_kernel,
        out_shape=jax.ShapeDtypeStruct((P, D), F32),
        grid_spec=grid_spec,
        compiler_params=pltpu.CompilerParams(
            dimension_semantics=("arbitrary",), vmem_limit_bytes=VMEM_LIMIT),
        name="moe",
    )(block_e, xbuf, gate_slot, p["moe_wg"], p["moe_wu"], p["moe_wd"])


def _route_and_moe(h2, logits, p):
    N, D = h2.shape
    glog = logits[:, :N_GROUPS]
    elog = logits[:, N_GROUPS:N_GROUPS + N_EXPERTS].reshape(N, N_GROUPS, EXPERTS_PER_GROUP)
    gprob = jax.nn.softmax(glog, axis=-1)
    grp = jnp.argmax(glog, axis=-1)
    p_grp = jnp.take_along_axis(gprob, grp[:, None], axis=-1)
    elog_g = jnp.take_along_axis(elog, grp[:, None, None], axis=1)[:, 0]
    top_v, top_i = lax.top_k(elog_g, 2)
    gate = (p_grp * jax.nn.softmax(top_v, axis=-1)).reshape(2 * N)
    eid = (grp[:, None] * EXPERTS_PER_GROUP + top_i).reshape(2 * N).astype(jnp.int32)

    M = 2 * N
    order = jnp.argsort(eid, stable=True).astype(jnp.int32)
    e_sorted = eid[order]
    experts = jnp.arange(N_EXPERTS, dtype=jnp.int32)
    counts = jnp.sum((eid[:, None] == experts[None, :]).astype(jnp.int32), axis=0)
    starts = jnp.cumsum(counts) - counts
    padded = (counts + MOE_ROWS - 1) // MOE_ROWS * MOE_ROWS
    pad_ends = jnp.cumsum(padded)
    pad_starts = pad_ends - padded
    n_blocks = (M + N_EXPERTS * (MOE_ROWS - 1) + MOE_ROWS - 1) // MOE_ROWS
    P = n_blocks * MOE_ROWS
    block_e = jnp.minimum(
        jnp.sum(pad_ends[None, :] <= (jnp.arange(n_blocks, dtype=jnp.int32) * MOE_ROWS)[:, None], axis=1),
        N_EXPERTS - 1).astype(jnp.int32)
    slot = jnp.arange(P, dtype=jnp.int32)
    e_slot = jnp.repeat(block_e, MOE_ROWS)
    off = slot - pad_starts[e_slot]
    valid = (off >= 0) & (off < counts[e_slot])
    src = jnp.clip(starts[e_slot] + off, 0, M - 1)
    assign = order[src]
    gate_slot = jnp.where(valid, gate[assign], 0.0)[:, None]
    xbuf = jnp.where(valid[:, None], h2[assign // 2], jnp.zeros((), h2.dtype))
    ybuf = _moe_call(block_e, xbuf, gate_slot, p)
    dest_sorted = pad_starts[e_sorted] + jnp.arange(M, dtype=jnp.int32) - starts[e_sorted]
    dest = jnp.zeros((M,), jnp.int32).at[order].set(dest_sorted)
    return ybuf[dest[0::2]] + ybuf[dest[1::2]]


def _final_kernel(x_ref, m_ref, w_ref, o_ref):
    x = x_ref[...] + m_ref[...]
    ms = jnp.mean(x * x, axis=-1, keepdims=True)
    o_ref[...] = x * lax.rsqrt(ms + NORM_EPS) * w_ref[...]


def _final_call(x1, moe, w):
    N, D = x1.shape
    rows = min(PROJ_ROWS * 2, N)
    tile = pl.BlockSpec((rows, D), lambda i: (i, 0))
    return pl.pallas_call(
        _final_kernel,
        out_shape=jax.ShapeDtypeStruct((N, D), F32),
        grid=(N // rows,),
        in_specs=[tile, tile, pl.BlockSpec((1, D), lambda i: (0, 0))],
        out_specs=tile,
        compiler_params=pltpu.CompilerParams(
            dimension_semantics=("parallel",), vmem_limit_bytes=VMEM_LIMIT),
        name="final",
    )(x1, moe, w)


def _block_diag_ones(width, head):
    idx = jnp.arange(width) // head
    return (idx[:, None] == idx[None, :]).astype(BF16)


def _prepare(norm1_w, w_in, mu_shift, rwkv_w0, rwkv_w2, rwkv_a0, rwkv_a2, rwkv_g2, rwkv_k_k, rwkv_k_a,
             rwkv_r_k, rwkv_lnx_w, rwkv_lnx_b, hgrn_lb, hgrn_gnorm_w, w_out, norm2_w, router_group_w,
             router_group_b, router_expert_w, router_expert_b, moe_w_gate, moe_w_up, moe_w_down, final_norm_w):
    row = lambda a: a.reshape(1, -1).astype(F32)
    lb_all = jnp.cumsum(jax.nn.softmax(hgrn_lb.astype(F32), axis=1), axis=1)
    rw = jnp.concatenate([router_group_w[0], router_expert_w[0]], axis=1)
    rw = jnp.pad(rw, ((0, 0), (0, ROUTER_PAD - rw.shape[1])))
    rw_hi = rw.astype(BF16)
    rw_lo = (rw - rw_hi.astype(F32)).astype(BF16)
    rb = jnp.pad(jnp.concatenate([router_group_b[0], router_expert_b[0]]), (0, ROUTER_PAD - N_GROUPS - N_EXPERTS))
    return {
        "norm1_w": row(norm1_w[0]),
        "w_r": w_in[0][:, :C_RWKV_END].astype(BF16),
        "w_h": w_in[0][:, C_RWKV_END:].astype(BF16),
        "mu": row(mu_shift[0]),
        "w0": rwkv_w0[0], "w2": rwkv_w2[0].astype(BF16),
        "a0": rwkv_a0[0], "a2": rwkv_a2[0].astype(BF16),
        "g2": rwkv_g2[0].astype(BF16),
        "k_k": row(rwkv_k_k[0]), "k_a": row(rwkv_k_a[0]), "r_k": row(rwkv_r_k[0]),
        "lnx_w": row(rwkv_lnx_w[0]), "lnx_b": row(rwkv_lnx_b[0]),
        "lb": lb_all[:, 0],
        "gnorm_w": row(jnp.tile(hgrn_gnorm_w[0], HGRN_HEADS)),
        "hd64": _block_diag_ones(D_RWKV, RWKV_HEAD),
        "hd128": _block_diag_ones(D_HGRN, HGRN_HEAD),
        "w_out": w_out[0].astype(BF16),
        "norm2_w": row(norm2_w[0]),
        "rw_hi": rw_hi, "rw_lo": rw_lo, "rb": row(rb),
        "moe_wg": moe_w_gate[0].astype(BF16), "moe_wu": moe_w_up[0].astype(BF16),
        "moe_wd": moe_w_down[0].astype(BF16),
        "final_w": row(final_norm_w),
    }


def _forward(x, p):
    B, T, D = x.shape
    r, v, kn, dec, kd, bd, g, bv, q, lf, kf, iv, sg = _proj_call(x, p)

    chains = B * RWKV_HEADS
    vq = LANES // chains
    nv = RWKV_HEAD // (vq * SUBLANES)
    ck = functools.partial(_to_chain_k, vq=vq)
    y = _rwkv_call(ck(r), ck(kn), _to_chain_v(v, vq, nv),
                   jnp.stack([ck(dec[0]), ck(dec[1])]), jnp.stack([ck(kd[0]), ck(kd[1])]),
                   jnp.stack([ck(bd[0]), ck(bd[1])]), nv)
    ya = _from_chain_v(y[0] + y[1], B, vq, nv)

    of, ob = _hgrn_call(q, iv, lf, kf)
    x1, h2, logits = _mix_call(x, ya, g, bv, of, ob, sg, p)
    N = B * T
    moe = _route_and_moe(h2.reshape(N, D), logits.reshape(N, ROUTER_PAD), p)
    return _final_call(x1.reshape(N, D), moe, p["final_w"]).reshape(B, T, D)


def kernel(x_prompt, x_sample, norm1_w, w_in, mu_shift, rwkv_w0, rwkv_w2, rwkv_a0, rwkv_a2, rwkv_g2, rwkv_k_k, rwkv_k_a, rwkv_r_k, rwkv_lnx_w, rwkv_lnx_b, hgrn_lb, hgrn_gnorm_w, w_out, norm2_w, router_group_w, router_group_b, router_expert_w, router_expert_b, moe_w_gate, moe_w_up, moe_w_down, final_norm_w):
    p = _prepare(norm1_w, w_in, mu_shift, rwkv_w0, rwkv_w2, rwkv_a0, rwkv_a2, rwkv_g2, rwkv_k_k, rwkv_k_a,
                 rwkv_r_k, rwkv_lnx_w, rwkv_lnx_b, hgrn_lb, hgrn_gnorm_w, w_out, norm2_w, router_group_w,
                 router_group_b, router_expert_w, router_expert_b, moe_w_gate, moe_w_up, moe_w_down,
                 final_norm_w)
    return (_forward(x_prompt, p), _forward(x_sample, p))
```

```python
import functools

import jax
import jax.numpy as jnp
from jax import lax
from jax.experimental import pallas as pl
from jax.experimental.pallas import tpu as pltpu

F32 = jnp.float32
BF16 = jnp.bfloat16

D_MODEL = 1024
D_RWKV = 512
RWKV_HEAD = 64
RWKV_HEADS = 8
D_HGRN = 512
HGRN_HEAD = 128
HGRN_HEADS = 4
DECAY_LORA = 64
AAA_LORA = 64
GATE_LORA = 128
N_GROUPS = 4
EXPERTS_PER_GROUP = 8
N_EXPERTS = 32
D_EXPERT = 512
NORM_EPS = 1e-6
HGRN_NORM_EPS = 1e-5
LNX_EPS = RWKV_HEAD * 1e-5

C_K = 512
C_V = 1024
C_WD = 1536
C_AD = 1664
C_GD = 1792
C_RWKV_END = 1920
D_IN = 4480
HG_Q = 0
HG_F = 512
HG_I = 1536
HG_G = 2048
D_HG_IN = 2560

SUBLANES = 8
LANES = 128
VMEM_LIMIT = 56 * 1024 * 1024

PROJ_ROWS = 256
HALO = 8
SCAN_STEPS = 32
HGRN_CHUNK = 64
HGRN_SUB = 16
HGRN_TILE = 256
MOE_ROWS = 256
ROUTER_PAD = 128


def _dot(a, b):
    return jnp.dot(a, b, preferred_element_type=F32)


def _dot_nt(a, b):
    return lax.dot_general(a, b, (((1,), (1,)), ((), ())), preferred_element_type=F32)


def _dot_tn(a, b):
    return lax.dot_general(a, b, (((0,), (0,)), ((), ())), preferred_element_type=F32)


def _split2(a):
    hi = a.astype(BF16)
    lo = (a - hi.astype(F32)).astype(BF16)
    return hi, lo


def _split3(a):
    hi = a.astype(BF16)
    r1 = a - hi.astype(F32)
    mid = r1.astype(BF16)
    lo = (r1 - mid.astype(F32)).astype(BF16)
    return hi, mid, lo


def _dot_lhs2(a, b_bf16):
    hi, lo = _split2(a)
    return _dot(hi, b_bf16) + _dot(lo, b_bf16)


def _sigmoid(x):
    return 1.0 / (1.0 + jnp.exp(-x))


def _softplus(x):
    return jnp.maximum(x, 0.0) + jnp.log(1.0 + jnp.exp(-jnp.abs(x)))


def _proj_kernel(x_ref, xp_ref, xn_ref, n1_ref, wr_ref, wh_ref, mu_ref, w0_ref, w2_ref, a0_ref, a2_ref,
                 g2_ref, kk_ref, ka_ref, rk_ref, lb_ref, hd_ref,
                 r_o, v_o, kn_o, dec_o, kd_o, bd_o, g_o, bv_o, q_o, lf_o, kf_o, iv_o, sg_o,
                 u_scr):
    i = pl.program_id(1)
    n_i = pl.num_programs(1)
    rows = x_ref.shape[1]

    xe = jnp.concatenate([xp_ref[0] * (i > 0).astype(F32), x_ref[0],
                          xn_ref[0] * (i < n_i - 1).astype(F32)], axis=0)
    ms = jnp.mean(xe * xe, axis=-1, keepdims=True)
    h_e = (xe * lax.rsqrt(ms + NORM_EPS) * n1_ref[...]).astype(BF16)

    u_scr[...] = _dot(h_e, wr_ref[...])
    u_c = u_scr[pl.ds(HALO, rows), :]
    u_prev = u_scr[pl.ds(HALO - 1, rows), :]
    u_next = u_scr[pl.ds(HALO + 1, rows), :]
    us = u_c + (0.5 * (u_prev + u_next) - u_c) * mu_ref[...]

    r = us[:, 0:C_K]
    k = us[:, C_K:C_V]
    v = us[:, C_V:C_WD]
    hd = hd_ref[...]
    kk = k * kk_ref[...]
    ss = _dot_lhs2(kk * kk, hd)
    kn = kk / jnp.maximum(jnp.sqrt(ss), 1e-12)
    gd = us[:, C_GD:C_RWKV_END]
    g_o[0] = _dot(_sigmoid(gd).astype(BF16), g2_ref[...])
    r_o[0] = r
    v_o[0] = v
    kn_o[0] = kn
    kd_sum = jnp.zeros_like(k)
    for d in range(2):
        wd = us[:, C_WD + d * DECAY_LORA:C_WD + (d + 1) * DECAY_LORA]
        ad = us[:, C_AD + d * AAA_LORA:C_AD + (d + 1) * AAA_LORA]
        zw = w0_ref[d:d + 1, :] + _dot(jnp.tanh(wd).astype(BF16), w2_ref[d])
        w_log = -_softplus(-zw) - 0.5
        dec_o[d, 0] = jnp.exp(-jnp.exp(w_log))
        a = _sigmoid(a0_ref[d:d + 1, :] + _dot(ad.astype(BF16), a2_ref[d]))
        kd = k * (1.0 + (a - 1.0) * ka_ref[...])
        kd_o[d, 0] = kd
        bd_o[d, 0] = kn * a
        kd_sum = kd_sum + kd
    bonus = _dot_lhs2(r * kd_sum * rk_ref[...], hd)
    bv_o[0] = bonus * v

    uh = _dot(h_e, wh_ref[...])[HALO:HALO + rows]
    uq = uh[:, HG_Q:HG_F]
    q_o[0] = uq * _sigmoid(uq)
    iv_o[0] = uh[:, HG_I:HG_G]
    ug = uh[:, HG_G:D_HG_IN]
    sg_o[0] = ug * _sigmoid(ug)
    for d in range(2):
        fr = uh[:, HG_F + d * D_HGRN:HG_F + (d + 1) * D_HGRN]
        lb = lb_ref[d:d + 1, :]
        f = lb + (1.0 - lb) * _sigmoid(fr)
        lf_o[d, 0] = jnp.log(f)
        kf_o[d, 0] = 1.0 - f


def _proj_call(x, p):
    B, T, D = x.shape
    rows = min(PROJ_ROWS, T)
    n_i = T // rows
    rb = rows // HALO

    def full(a):
        nd = a.ndim
        return pl.BlockSpec(a.shape, lambda b, i, _n=nd: (0,) * _n)

    tile = pl.BlockSpec((1, rows, D), lambda b, i: (b, i, 0))
    prev = pl.BlockSpec((1, HALO, D), lambda b, i: (b, jnp.maximum(i * rb - 1, 0), 0))
    nxt = pl.BlockSpec((1, HALO, D), lambda b, i: (b, jnp.minimum((i + 1) * rb, T // HALO - 1), 0))
    consts = [p["norm1_w"], p["w_r"], p["w_h"], p["mu"], p["w0"], p["w2"], p["a0"], p["a2"], p["g2"],
              p["k_k"], p["k_a"], p["r_k"], p["lb"], p["hd64"]]
    one = jax.ShapeDtypeStruct((B, T, D_RWKV), F32)
    two = jax.ShapeDtypeStruct((2, B, T, D_RWKV), F32)
    one_spec = pl.BlockSpec((1, rows, D_RWKV), lambda b, i: (b, i, 0))
    two_spec = pl.BlockSpec((2, 1, rows, D_RWKV), lambda b, i: (0, b, i, 0))
    kinds = [one, one, one, two, two, two, one, one, one, two, two, one, one]
    return pl.pallas_call(
        _proj_kernel,
        out_shape=kinds,
        grid=(B, n_i),
        in_specs=[tile, prev, nxt] + [full(a) for a in consts],
        out_specs=[one_spec if s is one else two_spec for s in kinds],
        scratch_shapes=[pltpu.VMEM((rows + 2 * HALO, C_RWKV_END), F32)],
        compiler_params=pltpu.CompilerParams(
            dimension_semantics=("parallel", "arbitrary"), vmem_limit_bytes=VMEM_LIMIT),
        name="proj",
    )(x, x, x, *consts)


def _rwkv_kernel(r_ref, kk_ref, v_ref, w_ref, k_ref, b_ref, y_ref, s_ref, *, nv, steps):
    d = pl.program_id(0)
    i = pl.program_id(1)
    K = RWKV_HEAD
    n_par = max(1, SUBLANES // nv)

    @pl.when(i == 0)
    def _():
        s_ref[...] = jnp.zeros_like(s_ref)

    def row(ref, t, k):
        return jnp.broadcast_to(ref[t, pl.ds(k, 1), :], (SUBLANES, LANES))

    def tree_sum(parts):
        parts = [x for x in parts if x is not None]
        while len(parts) > 1:
            parts = [parts[j] + parts[j + 1] if j + 1 < len(parts) else parts[j]
                     for j in range(0, len(parts), 2)]
        return parts[0]

    def step(j, carry):
        t = jnp.where(d == 0, j, steps - 1 - j)
        acc = [[None] * n_par for _ in range(nv)]
        for k in range(K):
            kk_row = row(kk_ref, t, k)
            for vg in range(nv):
                term = s_ref[vg * K + k] * kk_row
                slot = k % n_par
                acc[vg][slot] = term if acc[vg][slot] is None else acc[vg][slot] + term
        sa = [-tree_sum(acc[vg]) for vg in range(nv)]
        val = [v_ref[t, pl.ds(vg * SUBLANES, SUBLANES), :] for vg in range(nv)]
        yacc = [[None] * n_par for _ in range(nv)]
        for k in range(K):
            w_row = row(w_ref, t, k)
            b_row = row(b_ref, t, k)
            k_row = row(k_ref, t, k)
            r_row = row(r_ref, t, k)
            for vg in range(nv):
                s_new = s_ref[vg * K + k] * w_row + sa[vg] * b_row + val[vg] * k_row
                s_ref[vg * K + k] = s_new
                term = s_new * r_row
                slot = k % n_par
                yacc[vg][slot] = term if yacc[vg][slot] is None else yacc[vg][slot] + term
        for vg in range(nv):
            y_ref[t, pl.ds(vg * SUBLANES, SUBLANES), :] = tree_sum(yacc[vg])
        return carry

    lax.fori_loop(0, steps, step, 0)


def _rwkv_call(r, kk, v, w, k, b, nv):
    T = r.shape[0]
    steps = min(SCAN_STEPS, T)
    n_t = T // steps

    def tblk(d, i):
        return jnp.where(d == 0, i, n_t - 1 - i)

    shared = pl.BlockSpec((steps, RWKV_HEAD, LANES), lambda d, i: (tblk(d, i), 0, 0))
    vspec = pl.BlockSpec((steps, nv * SUBLANES, LANES), lambda d, i: (tblk(d, i), 0, 0))
    perdir = pl.BlockSpec((None, steps, RWKV_HEAD, LANES), lambda d, i: (d, tblk(d, i), 0, 0))
    yspec = pl.BlockSpec((None, steps, nv * SUBLANES, LANES), lambda d, i: (d, tblk(d, i), 0, 0))
    return pl.pallas_call(
        functools.partial(_rwkv_kernel, nv=nv, steps=steps),
        out_shape=jax.ShapeDtypeStruct((2, T, nv * SUBLANES, LANES), F32),
        grid=(2, n_t),
        in_specs=[shared, shared, vspec, perdir, perdir, perdir],
        out_specs=yspec,
        scratch_shapes=[pltpu.VMEM((nv * RWKV_HEAD, SUBLANES, LANES), F32)],
        compiler_params=pltpu.CompilerParams(
            dimension_semantics=("parallel", "arbitrary"), vmem_limit_bytes=VMEM_LIMIT),
        name="rwkv",
    )(r, kk, v, w, k, b)


def _to_chain_k(a, vq):
    B, T, _ = a.shape
    a = a.reshape(B, T, RWKV_HEADS, RWKV_HEAD)
    a = jnp.transpose(a, (1, 3, 0, 2)).reshape(T, RWKV_HEAD, B * RWKV_HEADS)
    return jnp.tile(a, (1, 1, vq)) if vq > 1 else a


def _to_chain_v(a, vq, nv):
    B, T, _ = a.shape
    a = a.reshape(B, T, RWKV_HEADS, vq, nv * SUBLANES)
    a = jnp.transpose(a, (1, 4, 3, 0, 2))
    return a.reshape(T, nv * SUBLANES, LANES)


def _from_chain_v(y, B, vq, nv):
    T = y.shape[0]
    y = y.reshape(T, nv * SUBLANES, vq, B, RWKV_HEADS)
    y = jnp.transpose(y, (3, 0, 4, 2, 1))
    return y.reshape(B, T, D_RWKV)


def _hgrn_direction(q, kf, val, lf, st_ref, tri, off_mask, sel, ones, rev, sub):
    C = q.shape[0]
    nb = C // sub
    l_hi, l_mid, l_lo = _split3(lf)
    G = _dot(tri, l_hi) + _dot(tri, l_mid) + _dot(tri, l_lo)
    last = 0 if rev else C - 1
    g_tot = G[last:last + 1, :]
    outs = []
    for h in range(HGRN_HEADS):
        sl = slice(h * HGRN_HEAD, (h + 1) * HGRN_HEAD)
        Gh, qh, kh, vh = G[:, sl], q[:, sl], kf[:, sl], val[:, sl]
        st = st_ref[h]
        o = _dot_nt((qh * jnp.exp(Gh)).astype(BF16), st.astype(BF16))
        rows = []
        for i in range(nb):
            if (i == nb - 1) if rev else (i == 0):
                rows.append(jnp.zeros((sub, C), F32))
                continue
            ref_row = (i + 1) * sub if rev else i * sub - 1
            ref = Gh[ref_row:ref_row + 1, :]
            qi = qh[i * sub:(i + 1) * sub] * jnp.exp(Gh[i * sub:(i + 1) * sub] - ref)
            ki = kh * jnp.exp(jnp.minimum(ref - Gh, 0.0))
            rows.append(_dot_nt(qi.astype(BF16), ki.astype(BF16)))
        a_off = jnp.concatenate(rows, axis=0) * off_mask
        o = o + _dot(a_off.astype(BF16), vh.astype(BF16))
        parts = []
        for t in range(C):
            b0 = (t // sub) * sub
            rel = jnp.exp(jnp.minimum(Gh[t:t + 1, :] - Gh[b0:b0 + sub, :], 0.0))
            parts.append(rel * kh[b0:b0 + sub, :] * qh[t:t + 1, :])
        ab = _dot(jnp.concatenate(parts, axis=0).astype(BF16), ones)
        wv = jnp.concatenate(
            [ab[t * sub:(t + 1) * sub] * vh[(t // sub) * sub:(t // sub + 1) * sub] for t in range(C)], axis=0)
        o = o + _dot(sel, wv.astype(BF16))
        outs.append(o)
        kdec = kh * jnp.exp(g_tot[:, sl] - Gh)
        st_ref[h] = st * jnp.exp(g_tot[:, sl]) + _dot_tn(vh.astype(BF16), kdec.astype(BF16))
    return jnp.concatenate(outs, axis=-1)


def _hgrn_kernel(qf_ref, vf_ref, lff_ref, kff_ref, qb_ref, vb_ref, lfb_ref, kfb_ref,
                 trif_ref, trib_ref, offf_ref, offb_ref, self_ref, selb_ref, ones_ref,
                 of_ref, ob_ref, stf_ref, stb_ref, *, chunk, sub):
    i = pl.program_id(1)
    n_c = qf_ref.shape[1] // chunk

    @pl.when(i == 0)
    def _():
        stf_ref[...] = jnp.zeros_like(stf_ref)
        stb_ref[...] = jnp.zeros_like(stb_ref)

    def body(c, carry):
        cf = pl.multiple_of(c * chunk, chunk)
        cb = pl.multiple_of((n_c - 1 - c) * chunk, chunk)
        of_ref[0, pl.ds(cf, chunk), :] = _hgrn_direction(
            qf_ref[0, pl.ds(cf, chunk), :], kff_ref[0, pl.ds(cf, chunk), :], vf_ref[0, pl.ds(cf, chunk), :],
            lff_ref[0, pl.ds(cf, chunk), :], stf_ref, trif_ref[...], offf_ref[...], self_ref[...], ones_ref[...],
            rev=False, sub=sub)
        ob_ref[0, pl.ds(cb, chunk), :] = _hgrn_direction(
            qb_ref[0, pl.ds(cb, chunk), :], kfb_ref[0, pl.ds(cb, chunk), :], vb_ref[0, pl.ds(cb, chunk), :],
            lfb_ref[0, pl.ds(cb, chunk), :], stb_ref, trib_ref[...], offb_ref[...], selb_ref[...], ones_ref[...],
            rev=True, sub=sub)
        return carry

    lax.fori_loop(0, n_c, body, 0)


def _hgrn_consts(chunk, sub):
    t = jnp.arange(chunk)
    tri_f = (t[None, :] <= t[:, None])
    blk = t // sub
    off_f = blk[None, :] < blk[:, None]
    col = jnp.arange(chunk * sub)
    same_t = (col // sub)[None, :] == t[:, None]
    j = (col % sub)[None, :]
    sel_f = same_t & (j <= (t % sub)[:, None])
    sel_b = same_t & (j >= (t % sub)[:, None])
    return [tri_f.astype(BF16), tri_f.T.astype(BF16), off_f.astype(F32), off_f.T.astype(F32),
            sel_f.astype(BF16), sel_b.astype(BF16), jnp.ones((HGRN_HEAD, HGRN_HEAD), BF16)]


def _hgrn_call(q, iv, lf, kf):
    B, T, _ = q.shape
    tile = min(HGRN_TILE, T)
    chunk = min(HGRN_CHUNK, tile)
    sub = min(HGRN_SUB, chunk)
    n_i = T // tile
    consts = _hgrn_consts(chunk, sub)
    fwd = pl.BlockSpec((1, tile, D_HGRN), lambda b, i: (b, i, 0))
    bwd = pl.BlockSpec((1, tile, D_HGRN), lambda b, i: (b, n_i - 1 - i, 0))
    fwd2 = pl.BlockSpec((None, 1, tile, D_HGRN), lambda b, i: (0, b, i, 0))
    bwd2 = pl.BlockSpec((None, 1, tile, D_HGRN), lambda b, i: (1, b, n_i - 1 - i, 0))
    full = [pl.BlockSpec(a.shape, lambda b, i: (0, 0)) for a in consts]
    out = jax.ShapeDtypeStruct((B, T, D_HGRN), F32)
    state = pltpu.VMEM((HGRN_HEADS, HGRN_HEAD, HGRN_HEAD), F32)
    return pl.pallas_call(
        functools.partial(_hgrn_kernel, chunk=chunk, sub=sub),
        out_shape=[out, out],
        grid=(B, n_i),
        in_specs=[fwd, fwd, fwd2, fwd2, bwd, bwd, bwd2, bwd2] + full,
        out_specs=[fwd, bwd],
        scratch_shapes=[state, state],
        compiler_params=pltpu.CompilerParams(
            dimension_semantics=("parallel", "arbitrary"), vmem_limit_bytes=VMEM_LIMIT),
        name="hgrn",
    )(q, iv, lf, kf, q, iv, lf, kf, *consts)


def _mix_kernel(x_ref, ya_ref, g_ref, bv_ref, of_ref, ob_ref, sg_ref, lnw_ref, lnb_ref, gnw_ref,
                hd64_ref, hd128_ref, wo_ref, n2_ref, rwh_ref, rwl_ref, rb_ref,
                x1_o, h2_o, lg_o):
    hd64 = hd64_ref[...]
    y = ya_ref[0]
    inv = 1.0 / RWKV_HEAD
    mean = _dot_lhs2(y, hd64) * inv
    yc = y - mean
    var = _dot_lhs2(yc * yc, hd64) * inv
    ya = (yc * lax.rsqrt(var + LNX_EPS)) * lnw_ref[...] + lnb_ref[...]
    ya = (ya + bv_ref[0]) * g_ref[0]
    o = of_ref[0] + ob_ref[0]
    ms = _dot_lhs2(o * o, hd128_ref[...]) * (1.0 / HGRN_HEAD)
    yb = o * lax.rsqrt(ms + HGRN_NORM_EPS) * gnw_ref[...] * sg_ref[0]
    mixed = jnp.concatenate([ya, yb], axis=-1).astype(BF16)
    x1 = x_ref[0] + _dot(mixed, wo_ref[...])
    x1_o[0] = x1
    ms2 = jnp.mean(x1 * x1, axis=-1, keepdims=True)
    h2 = x1 * lax.rsqrt(ms2 + NORM_EPS) * n2_ref[...]
    h2_o[0] = h2
    h_hi, h_lo = _split2(h2)
    w_hi = rwh_ref[...]
    lg_o[0] = _dot(h_hi, w_hi) + _dot(h_lo, w_hi) + _dot(h_hi, rwl_ref[...]) + rb_ref[...]


def _mix_call(x, ya, g, bv, of, ob, sg, p):
    B, T, D = x.shape
    rows = min(PROJ_ROWS, T)

    def full(a):
        nd = a.ndim
        return pl.BlockSpec(a.shape, lambda b, i, _n=nd: (0,) * _n)

    def tile(w):
        return pl.BlockSpec((1, rows, w), lambda b, i: (b, i, 0))

    consts = [p["lnx_w"], p["lnx_b"], p["gnorm_w"], p["hd64"], p["hd128"], p["w_out"], p["norm2_w"],
              p["rw_hi"], p["rw_lo"], p["rb"]]
    return pl.pallas_call(
        _mix_kernel,
        out_shape=[jax.ShapeDtypeStruct((B, T, D), F32), jax.ShapeDtypeStruct((B, T, D), F32),
                   jax.ShapeDtypeStruct((B, T, ROUTER_PAD), F32)],
        grid=(B, T // rows),
        in_specs=[tile(D)] + [tile(D_RWKV)] * 6 + [full(a) for a in consts],
        out_specs=[tile(D), tile(D), tile(ROUTER_PAD)],
        compiler_params=pltpu.CompilerParams(
            dimension_semantics=("parallel", "parallel"), vmem_limit_bytes=VMEM_LIMIT),
        name="mix",
    )(x, ya, g, bv, of, ob, sg, *consts)


def _moe_kernel(be_ref, nu_ref, tok0_ref, tokn_ref, dst_ref, gt_ref, h_hbm, wg_ref, wu_ref, wd_ref, y_hbm,
                xbuf, ybuf, gsem, ssem):
    del be_ref
    i = pl.program_id(0)
    nb_used = nu_ref[0]
    slot = lax.rem(i, 2)

    def gather_copy(tok, r, s):
        return pltpu.make_async_copy(h_hbm.at[pl.ds(tok, 1)], xbuf.at[s, pl.ds(r, 1)], gsem.at[s])

    def scatter_copy(dst, r, s):
        return pltpu.make_async_copy(ybuf.at[s, pl.ds(r, 1)], y_hbm.at[pl.ds(dst, 1)], ssem.at[s])

    def start_gather(tok_ref, s):
        def body(r, c):
            gather_copy(tok_ref[0, 0, r], r, s).start()
            return c
        lax.fori_loop(0, MOE_ROWS, body, 0, unroll=8)

    def start_scatter(s):
        def body(r, c):
            scatter_copy(dst_ref[0, 0, r], r, s).start()
            return c
        lax.fori_loop(0, MOE_ROWS, body, 0, unroll=8)

    def wait_gather(s):
        pltpu.make_async_copy(h_hbm.at[pl.ds(0, MOE_ROWS)], xbuf.at[s], gsem.at[s]).wait()

    def wait_scatter(s):
        pltpu.make_async_copy(ybuf.at[s], y_hbm.at[pl.ds(0, MOE_ROWS)], ssem.at[s]).wait()

    @pl.when(i == 0)
    def _():
        n_res = y_hbm.shape[0] - 2 * MOE_ROWS
        ybuf[...] = jnp.zeros_like(ybuf)
        for s in range(2):
            spare = pltpu.make_async_copy(ybuf.at[s], y_hbm.at[pl.ds(n_res + s * MOE_ROWS, MOE_ROWS)], ssem.at[s])
            spare.start()
            spare.wait()

    @pl.when(jnp.logical_and(i == 0, nb_used > 0))
    def _():
        start_gather(tok0_ref, 0)

    @pl.when(i + 1 < nb_used)
    def _():
        start_gather(tokn_ref, 1 - slot)

    @pl.when(i < nb_used)
    def _():
        wait_gather(slot)

        @pl.when(i >= 2)
        def _():
            wait_scatter(slot)

        x = xbuf[slot].astype(BF16)
        a = _dot(x, wg_ref[0])
        u = _dot(x, wu_ref[0])
        hid = (a * _sigmoid(a)) * u
        ybuf[slot] = _dot(hid.astype(BF16), wd_ref[0]) * gt_ref[...]
        start_scatter(slot)

        @pl.when(i == nb_used - 1)
        def _():
            wait_scatter(slot)

            @pl.when(i >= 1)
            def _():
                wait_scatter(1 - slot)


def _moe_call(block_e, nb_used, src_tok, dst_row, gate_slot, h2, p):
    N, D = h2.shape
    P = src_tok.shape[0]
    n_blocks = P // MOE_ROWS
    smem_blk = lambda f: pl.BlockSpec((1, 1, MOE_ROWS), f, memory_space=pltpu.SMEM)
    grid_spec = pltpu.PrefetchScalarGridSpec(
        num_scalar_prefetch=2,
        grid=(n_blocks,),
        in_specs=[
            smem_blk(lambda i, be, nu: (0, 0, 0)),
            smem_blk(lambda i, be, nu: (jnp.minimum(i + 1, n_blocks - 1), 0, 0)),
            smem_blk(lambda i, be, nu: (i, 0, 0)),
            pl.BlockSpec((MOE_ROWS, 1), lambda i, be, nu: (i, 0)),
            pl.BlockSpec(memory_space=pl.ANY),
            pl.BlockSpec((1, D, D_EXPERT), lambda i, be, nu: (be[i], 0, 0)),
            pl.BlockSpec((1, D, D_EXPERT), lambda i, be, nu: (be[i], 0, 0)),
            pl.BlockSpec((1, D_EXPERT, D), lambda i, be, nu: (be[i], 0, 0)),
        ],
        out_specs=pl.BlockSpec(memory_space=pl.ANY),
        scratch_shapes=[pltpu.VMEM((2, MOE_ROWS, D), F32), pltpu.VMEM((2, MOE_ROWS, D), F32),
                        pltpu.SemaphoreType.DMA((2,)), pltpu.SemaphoreType.DMA((2,))],
    )
    tok3 = src_tok.reshape(n_blocks, 1, MOE_ROWS)
    return pl.pallas_call(
        _moe_kernel,
        out_shape=jax.ShapeDtypeStruct((2 * N + 2 * MOE_ROWS, D), F32),
        grid_spec=grid_spec,
        compiler_params=pltpu.CompilerParams(
            dimension_semantics=("arbitrary",), vmem_limit_bytes=VMEM_LIMIT),
        name="moe",
    )(block_e, nb_used, tok3, tok3, dst_row.reshape(n_blocks, 1, MOE_ROWS), gate_slot, h2,
      p["moe_wg"], p["moe_wu"], p["moe_wd"])


def _route_and_moe(h2, logits, p):
    N, D = h2.shape
    glog = logits[:, :N_GROUPS]
    elog = logits[:, N_GROUPS:N_GROUPS + N_EXPERTS].reshape(N, N_GROUPS, EXPERTS_PER_GROUP)
    gprob = jax.nn.softmax(glog, axis=-1)
    grp = jnp.argmax(glog, axis=-1)
    p_grp = jnp.take_along_axis(gprob, grp[:, None], axis=-1)
    elog_g = jnp.take_along_axis(elog, grp[:, None, None], axis=1)[:, 0]
    top_v, top_i = lax.top_k(elog_g, 2)
    gate = (p_grp * jax.nn.softmax(top_v, axis=-1)).reshape(2 * N)
    eid = (grp[:, None] * EXPERTS_PER_GROUP + top_i).reshape(2 * N).astype(jnp.int32)

    M = 2 * N
    order = jnp.argsort(eid, stable=True).astype(jnp.int32)
    experts = jnp.arange(N_EXPERTS, dtype=jnp.int32)
    counts = jnp.sum((eid[:, None] == experts[None, :]).astype(jnp.int32), axis=0)
    starts = jnp.cumsum(counts) - counts
    padded = (counts + MOE_ROWS - 1) // MOE_ROWS * MOE_ROWS
    pad_ends = jnp.cumsum(padded)
    pad_starts = pad_ends - padded
    n_blocks = (M + N_EXPERTS * (MOE_ROWS - 1) + MOE_ROWS - 1) // MOE_ROWS
    P = n_blocks * MOE_ROWS
    block_e = jnp.minimum(
        jnp.sum(pad_ends[None, :] <= (jnp.arange(n_blocks, dtype=jnp.int32) * MOE_ROWS)[:, None], axis=1),
        N_EXPERTS - 1).astype(jnp.int32)
    slot = jnp.arange(P, dtype=jnp.int32)
    e_slot = jnp.repeat(block_e, MOE_ROWS)
    off = slot - pad_starts[e_slot]
    valid = (off >= 0) & (off < counts[e_slot])
    src = jnp.clip(starts[e_slot] + off, 0, M - 1)
    assign = order[src]
    gate_slot = jnp.where(valid, gate[assign], 0.0)[:, None]
    src_tok = jnp.where(valid, assign // 2, 0)
    spare = M + ((slot // MOE_ROWS) % 2) * MOE_ROWS + slot % MOE_ROWS
    dst_row = jnp.where(valid, assign, spare)
    nb_used = (pad_ends[-1] // MOE_ROWS).astype(jnp.int32).reshape(1)
    return _moe_call(block_e, nb_used, src_tok, dst_row, gate_slot, h2, p)


def _final_kernel(x_ref, m_ref, w_ref, o_ref):
    D = x_ref.shape[1]
    x = x_ref[...] + m_ref[:, 0:D] + m_ref[:, D:2 * D]
    ms = jnp.mean(x * x, axis=-1, keepdims=True)
    o_ref[...] = x * lax.rsqrt(ms + NORM_EPS) * w_ref[...]


def _final_call(x1, moe2, w):
    N, D = x1.shape
    rows = min(PROJ_ROWS * 2, N)
    tile = pl.BlockSpec((rows, D), lambda i: (i, 0))
    pair = pl.BlockSpec((rows, 2 * D), lambda i: (i, 0))
    moe = moe2.reshape(moe2.shape[0] // 2, 2 * D)
    return pl.pallas_call(
        _final_kernel,
        out_shape=jax.ShapeDtypeStruct((N, D), F32),
        grid=(N // rows,),
        in_specs=[tile, pair, pl.BlockSpec((1, D), lambda i: (0, 0))],
        out_specs=tile,
        compiler_params=pltpu.CompilerParams(
            dimension_semantics=("parallel",), vmem_limit_bytes=VMEM_LIMIT),
        name="final",
    )(x1, moe, w)


def _block_diag_ones(width, head):
    idx = jnp.arange(width) // head
    return (idx[:, None] == idx[None, :]).astype(BF16)


def _prepare(norm1_w, w_in, mu_shift, rwkv_w0, rwkv_w2, rwkv_a0, rwkv_a2, rwkv_g2, rwkv_k_k, rwkv_k_a,
             rwkv_r_k, rwkv_lnx_w, rwkv_lnx_b, hgrn_lb, hgrn_gnorm_w, w_out, norm2_w, router_group_w,
             router_group_b, router_expert_w, router_expert_b, moe_w_gate, moe_w_up, moe_w_down, final_norm_w):
    row = lambda a: a.reshape(1, -1).astype(F32)
    lb_all = jnp.cumsum(jax.nn.softmax(hgrn_lb.astype(F32), axis=1), axis=1)
    rw = jnp.concatenate([router_group_w[0], router_expert_w[0]], axis=1)
    rw = jnp.pad(rw, ((0, 0), (0, ROUTER_PAD - rw.shape[1])))
    rw_hi = rw.astype(BF16)
    rw_lo = (rw - rw_hi.astype(F32)).astype(BF16)
    rb = jnp.pad(jnp.concatenate([router_group_b[0], router_expert_b[0]]), (0, ROUTER_PAD - N_GROUPS - N_EXPERTS))
    return {
        "norm1_w": row(norm1_w[0]),
        "w_r": w_in[0][:, :C_RWKV_END].astype(BF16),
        "w_h": w_in[0][:, C_RWKV_END:].astype(BF16),
        "mu": row(mu_shift[0]),
        "w0": rwkv_w0[0], "w2": rwkv_w2[0].astype(BF16),
        "a0": rwkv_a0[0], "a2": rwkv_a2[0].astype(BF16),
        "g2": rwkv_g2[0].astype(BF16),
        "k_k": row(rwkv_k_k[0]), "k_a": row(rwkv_k_a[0]), "r_k": row(rwkv_r_k[0]),
        "lnx_w": row(rwkv_lnx_w[0]), "lnx_b": row(rwkv_lnx_b[0]),
        "lb": lb_all[:, 0],
        "gnorm_w": row(jnp.tile(hgrn_gnorm_w[0], HGRN_HEADS)),
        "hd64": _block_diag_ones(D_RWKV, RWKV_HEAD),
        "hd128": _block_diag_ones(D_HGRN, HGRN_HEAD),
        "w_out": w_out[0].astype(BF16),
        "norm2_w": row(norm2_w[0]),
        "rw_hi": rw_hi, "rw_lo": rw_lo, "rb": row(rb),
        "moe_wg": moe_w_gate[0].astype(BF16), "moe_wu": moe_w_up[0].astype(BF16),
        "moe_wd": moe_w_down[0].astype(BF16),
        "final_w": row(final_norm_w),
    }


def _forward(x, p):
    B, T, D = x.shape
    r, v, kn, dec, kd, bd, g, bv, q, lf, kf, iv, sg = _proj_call(x, p)

    chains = B * RWKV_HEADS
    vq = LANES // chains
    nv = RWKV_HEAD // (vq * SUBLANES)
    ck = functools.partial(_to_chain_k, vq=vq)
    y = _rwkv_call(ck(r), ck(kn), _to_chain_v(v, vq, nv),
                   jnp.stack([ck(dec[0]), ck(dec[1])]), jnp.stack([ck(kd[0]), ck(kd[1])]),
                   jnp.stack([ck(bd[0]), ck(bd[1])]), nv)
    ya = _from_chain_v(y[0] + y[1], B, vq, nv)

    of, ob = _hgrn_call(q, iv, lf, kf)
    x1, h2, logits = _mix_call(x, ya, g, bv, of, ob, sg, p)
    N = B * T
    moe = _route_and_moe(h2.reshape(N, D), logits.reshape(N, ROUTER_PAD), p)
    return _final_call(x1.reshape(N, D), moe, p["final_w"]).reshape(B, T, D)


def kernel(x_prompt, x_sample, norm1_w, w_in, mu_shift, rwkv_w0, rwkv_w2, rwkv_a0, rwkv_a2, rwkv_g2, rwkv_k_k, rwkv_k_a, rwkv_r_k, rwkv_lnx_w, rwkv_lnx_b, hgrn_lb, hgrn_gnorm_w, w_out, norm2_w, router_group_w, router_group_b, router_expert_w, router_expert_b, moe_w_gate, moe_w_up, moe_w_down, final_norm_w):
    p = _prepare(norm1_w, w_in, mu_shift, rwkv_w0, rwkv_w2, rwkv_a0, rwkv_a2, rwkv_g2, rwkv_k_k, rwkv_k_a,
                 rwkv_r_k, rwkv_lnx_w, rwkv_lnx_b, hgrn_lb, hgrn_gnorm_w, w_out, norm2_w, router_group_w,
                 router_group_b, router_expert_w, router_expert_b, moe_w_gate, moe_w_up, moe_w_down,
                 final_norm_w)
    return (_forward(x_prompt, p), _forward(x_sample, p))
```

```python
import functools

import jax
import jax.numpy as jnp
from jax import lax
from jax.experimental import pallas as pl
from jax.experimental.pallas import tpu as pltpu

F32 = jnp.float32
BF16 = jnp.bfloat16

D_MODEL = 1024
D_RWKV = 512
RWKV_HEAD = 64
RWKV_HEADS = 8
D_HGRN = 512
HGRN_HEAD = 128
HGRN_HEADS = 4
DECAY_LORA = 64
AAA_LORA = 64
GATE_LORA = 128
N_GROUPS = 4
EXPERTS_PER_GROUP = 8
N_EXPERTS = 32
D_EXPERT = 512
NORM_EPS = 1e-6
HGRN_NORM_EPS = 1e-5
LNX_EPS = RWKV_HEAD * 1e-5

C_K = 512
C_V = 1024
C_WD = 1536
C_AD = 1664
C_GD = 1792
C_RWKV_END = 1920
D_IN = 4480
HG_Q = 0
HG_F = 512
HG_I = 1536
HG_G = 2048
D_HG_IN = 2560

SUBLANES = 8
LANES = 128
VMEM_LIMIT = 56 * 1024 * 1024

PROJ_ROWS = 256
HALO = 8
SCAN_STEPS = 32
HGRN_CHUNK = 64
HGRN_SUB = 16
HGRN_TILE = 256
MOE_ROWS = 256
ROUTER_PAD = 128
RELAYOUT_T = 128

Z_R, Z_KK, Z_W, Z_K, Z_B, Z_V = 0, 1, 2, 4, 6, 8
Z_COUNT = 9


def _dot(a, b):
    return jnp.dot(a, b, preferred_element_type=F32)


def _dot_nt(a, b):
    return lax.dot_general(a, b, (((1,), (1,)), ((), ())), preferred_element_type=F32)


def _dot_tn(a, b):
    return lax.dot_general(a, b, (((0,), (0,)), ((), ())), preferred_element_type=F32)


def _split2(a):
    hi = a.astype(BF16)
    lo = (a - hi.astype(F32)).astype(BF16)
    return hi, lo


def _split3(a):
    hi = a.astype(BF16)
    r1 = a - hi.astype(F32)
    mid = r1.astype(BF16)
    lo = (r1 - mid.astype(F32)).astype(BF16)
    return hi, mid, lo


def _dot_lhs2(a, b_bf16):
    hi, lo = _split2(a)
    return _dot(hi, b_bf16) + _dot(lo, b_bf16)


def _sigmoid(x):
    return 1.0 / (1.0 + jnp.exp(-x))


def _softplus(x):
    return jnp.maximum(x, 0.0) + jnp.log(1.0 + jnp.exp(-jnp.abs(x)))


def _lanes(c, n):
    if n == LANES:
        return c
    if n < LANES:
        return c[:, :n]
    return jnp.concatenate([c] * (n // LANES), axis=1)


def _dot_rhs2(a_bf16, b):
    hi, lo = _split2(b)
    return _dot(a_bf16, hi) + _dot(a_bf16, lo)


def _proj_kernel(x_ref, xp_ref, xn_ref, n1_ref, wr_ref, wh_ref, mu_ref, w0_ref, w2_ref, a0_ref, a2_ref,
                 g2_ref, kk_ref, ka_ref, rk_ref, lb_ref, hd_ref,
                 z_o, g_o, bv_o, q_o, lf_o, kf_o, iv_o, sg_o):
    i = pl.program_id(1)
    n_i = pl.num_programs(1)
    rows = x_ref.shape[1]

    def norm(xf):
        ms = jnp.mean(xf * xf, axis=-1, keepdims=True)
        return (xf * lax.rsqrt(ms + NORM_EPS) * n1_ref[...]).astype(BF16)

    h_c = norm(x_ref[0])
    h_h = norm(jnp.concatenate([xp_ref[0] * (i > 0).astype(F32), xn_ref[0] * (i < n_i - 1).astype(F32)], axis=0))

    wr = wr_ref[...]
    u_c = _dot_nt(wr, h_c)
    u_h = _dot_nt(wr, h_h)
    lane = lax.broadcasted_iota(jnp.int32, (1, rows), 1)
    u_prev = jnp.where(lane == 0, u_h[:, HALO - 1:HALO], pltpu.roll(u_c, 1, 1))
    u_next = jnp.where(lane == rows - 1, u_h[:, HALO:HALO + 1], pltpu.roll(u_c, rows - 1, 1))
    us = u_c + (0.5 * (u_prev + u_next) - u_c) * _lanes(mu_ref[...], rows)

    def put(j, a):
        z_o[j] = a.reshape(RWKV_HEAD, SUBLANES, rows)

    r = us[0:C_K]
    k = us[C_K:C_V]
    v = us[C_V:C_WD]
    hd = hd_ref[...]
    kk = k * _lanes(kk_ref[...], rows)
    kn = kk / jnp.maximum(jnp.sqrt(_dot_rhs2(hd, kk * kk)), 1e-12)
    gd = us[C_GD:C_RWKV_END]
    g_o[0] = _dot_tn(_sigmoid(gd).astype(BF16), g2_ref[...])
    put(Z_R, r)
    put(Z_KK, kn)
    put(Z_V, v)
    kd_sum = jnp.zeros_like(k)
    for d in range(2):
        wd = us[C_WD + d * DECAY_LORA:C_WD + (d + 1) * DECAY_LORA]
        ad = us[C_AD + d * AAA_LORA:C_AD + (d + 1) * AAA_LORA]
        zw = _lanes(w0_ref[d], rows) + _dot(w2_ref[d], jnp.tanh(wd).astype(BF16))
        w_log = -_softplus(-zw) - 0.5
        put(Z_W + d, jnp.exp(-jnp.exp(w_log)))
        a = _sigmoid(_lanes(a0_ref[d], rows) + _dot(a2_ref[d], ad.astype(BF16)))
        kd = k * (1.0 + (a - 1.0) * _lanes(ka_ref[...], rows))
        put(Z_K + d, kd)
        put(Z_B + d, kn * a)
        kd_sum = kd_sum + kd
    bonus = _dot_rhs2(hd, r * kd_sum * _lanes(rk_ref[...], rows))
    bv_o[0] = (bonus * v).T

    uh = _dot(h_c, wh_ref[...])
    uq = uh[:, HG_Q:HG_F]
    q_o[0] = uq * _sigmoid(uq)
    iv_o[0] = uh[:, HG_I:HG_G]
    ug = uh[:, HG_G:D_HG_IN]
    sg_o[0] = ug * _sigmoid(ug)
    for d in range(2):
        fr = uh[:, HG_F + d * D_HGRN:HG_F + (d + 1) * D_HGRN]
        lb = lb_ref[d:d + 1, :]
        f = lb + (1.0 - lb) * _sigmoid(fr)
        lf_o[d, 0] = jnp.log(f)
        kf_o[d, 0] = 1.0 - f


def _proj_call(x, p):
    B, T, D = x.shape
    rows = min(PROJ_ROWS, T)
    n_i = T // rows
    rb = rows // HALO

    def full(a):
        nd = a.ndim
        return pl.BlockSpec(a.shape, lambda b, i, _n=nd: (0,) * _n)

    tile = pl.BlockSpec((1, rows, D), lambda b, i: (b, i, 0))
    prev = pl.BlockSpec((1, HALO, D), lambda b, i: (b, jnp.maximum(i * rb - 1, 0), 0))
    nxt = pl.BlockSpec((1, HALO, D), lambda b, i: (b, jnp.minimum((i + 1) * rb, T // HALO - 1), 0))
    consts = [p["norm1_w"], p["w_r"], p["w_h"], p["mu"], p["w0"], p["w2"], p["a0"], p["a2"], p["g2"],
              p["k_k"], p["k_a"], p["r_k"], p["lb"], p["hd64"]]
    one = jax.ShapeDtypeStruct((B, T, D_RWKV), F32)
    two = jax.ShapeDtypeStruct((2, B, T, D_RWKV), F32)
    one_spec = pl.BlockSpec((1, rows, D_RWKV), lambda b, i: (b, i, 0))
    two_spec = pl.BlockSpec((2, 1, rows, D_RWKV), lambda b, i: (0, b, i, 0))
    z = jax.ShapeDtypeStruct((Z_COUNT, RWKV_HEAD, B * RWKV_HEADS, T), F32)
    z_spec = pl.BlockSpec((Z_COUNT, RWKV_HEAD, RWKV_HEADS, rows), lambda b, i: (0, 0, b, i))
    kinds = [z, one, one, one, two, two, one, one]
    return pl.pallas_call(
        _proj_kernel,
        out_shape=kinds,
        grid=(B, n_i),
        in_specs=[tile, prev, nxt] + [full(a) for a in consts],
        out_specs=[z_spec if s is z else (one_spec if s is one else two_spec) for s in kinds],
        compiler_params=pltpu.CompilerParams(
            dimension_semantics=("parallel", "arbitrary"), vmem_limit_bytes=VMEM_LIMIT),
        name="proj",
    )(x, x, x, *consts)


def _rwkv_kernel(r_ref, kk_ref, v_ref, w_ref, k_ref, b_ref, y_ref, s_ref, *, nv, steps):
    d = pl.program_id(0)
    i = pl.program_id(1)
    K = RWKV_HEAD
    n_par = max(1, SUBLANES // nv)

    @pl.when(i == 0)
    def _():
        s_ref[...] = jnp.zeros_like(s_ref)

    def row(ref, tt, k):
        return jnp.broadcast_to(ref[tt[0], k, pl.ds(tt[1], 1), :], (SUBLANES, LANES))

    def tree_sum(parts):
        parts = [x for x in parts if x is not None]
        while len(parts) > 1:
            parts = [parts[j] + parts[j + 1] if j + 1 < len(parts) else parts[j]
                     for j in range(0, len(parts), 2)]
        return parts[0]

    def step(j, carry):
        t = jnp.where(d == 0, j, steps - 1 - j)
        tt = (t // SUBLANES, t % SUBLANES)
        acc = [[None] * n_par for _ in range(nv)]
        for k in range(K):
            kk_row = row(kk_ref, tt, k)
            for vg in range(nv):
                term = s_ref[vg * K + k] * kk_row
                slot = k % n_par
                acc[vg][slot] = term if acc[vg][slot] is None else acc[vg][slot] + term
        sa = [-tree_sum(acc[vg]) for vg in range(nv)]
        val = [v_ref[t, pl.ds(vg * SUBLANES, SUBLANES), :] for vg in range(nv)]
        yacc = [[None] * n_par for _ in range(nv)]
        for k in range(K):
            w_row = row(w_ref, tt, k)
            b_row = row(b_ref, tt, k)
            k_row = row(k_ref, tt, k)
            r_row = row(r_ref, tt, k)
            for vg in range(nv):
                s_new = s_ref[vg * K + k] * w_row + sa[vg] * b_row + val[vg] * k_row
                s_ref[vg * K + k] = s_new
                term = s_new * r_row
                slot = k % n_par
                yacc[vg][slot] = term if yacc[vg][slot] is None else yacc[vg][slot] + term
        for vg in range(nv):
            y_ref[t, pl.ds(vg * SUBLANES, SUBLANES), :] = tree_sum(yacc[vg])
        return carry

    lax.fori_loop(0, steps, step, 0)


def _rwkv_call(a, v, nv):
    T = a.shape[1] * SUBLANES
    steps = min(SCAN_STEPS, T)
    n_t = T // steps
    nvp = nv * SUBLANES

    def tblk(d, i):
        return jnp.where(d == 0, i, n_t - 1 - i)

    kblock = (None, steps // SUBLANES, RWKV_HEAD, SUBLANES, LANES)

    def shared(j):
        return pl.BlockSpec(kblock, lambda d, i: (j, tblk(d, i), 0, 0, 0))

    def perdir(j):
        return pl.BlockSpec(kblock, lambda d, i: (j + d, tblk(d, i), 0, 0, 0))

    vspec = pl.BlockSpec((steps, nvp, LANES), lambda d, i: (tblk(d, i), 0, 0))
    yspec = pl.BlockSpec((None, steps, nvp, LANES), lambda d, i: (d, tblk(d, i), 0, 0))
    return pl.pallas_call(
        functools.partial(_rwkv_kernel, nv=nv, steps=steps),
        out_shape=jax.ShapeDtypeStruct((2, T, nvp, LANES), F32),
        grid=(2, n_t),
        in_specs=[shared(Z_R), shared(Z_KK), vspec, perdir(Z_W), perdir(Z_K), perdir(Z_B)],
        out_specs=yspec,
        scratch_shapes=[pltpu.VMEM((nv * RWKV_HEAD, SUBLANES, LANES), F32)],
        compiler_params=pltpu.CompilerParams(
            dimension_semantics=("parallel", "arbitrary"), vmem_limit_bytes=VMEM_LIMIT),
        name="rwkv",
    )(a, a, v, a, a, a)


def _to_chains_k_kernel(z_ref, o_ref, *, vq):
    tl = z_ref.shape[2]
    for p in range(RWKV_HEAD):
        a = z_ref[p]
        a = (jnp.concatenate([a] * vq, axis=0) if vq > 1 else a).T
        o_ref[:, p] = a.reshape(tl // SUBLANES, SUBLANES, LANES)


def _to_chains_v_kernel(z_ref, o_ref, *, vq, nvp):
    for p in range(nvp):
        o_ref[:, p, :] = jnp.concatenate([z_ref[q * nvp + p] for q in range(vq)], axis=0).T


def _to_chains_call(z, vq, nvp):
    n_arr, planes, C, T = z.shape
    tl = min(RELAYOUT_T, T)
    a = pl.pallas_call(
        functools.partial(_to_chains_k_kernel, vq=vq),
        out_shape=jax.ShapeDtypeStruct((Z_V, T // SUBLANES, planes, SUBLANES, LANES), F32),
        grid=(Z_V, T // tl),
        in_specs=[pl.BlockSpec((None, planes, C, tl), lambda j, t: (j, 0, 0, t))],
        out_specs=pl.BlockSpec((None, tl // SUBLANES, planes, SUBLANES, LANES), lambda j, t: (j, t, 0, 0, 0)),
        compiler_params=pltpu.CompilerParams(
            dimension_semantics=("parallel", "parallel"), vmem_limit_bytes=VMEM_LIMIT),
        name="to_chains_k",
    )(z)
    v = pl.pallas_call(
        functools.partial(_to_chains_v_kernel, vq=vq, nvp=nvp),
        out_shape=jax.ShapeDtypeStruct((T, nvp, LANES), F32),
        grid=(T // tl,),
        in_specs=[pl.BlockSpec((None, planes, C, tl), lambda t: (Z_V, 0, 0, t))],
        out_specs=pl.BlockSpec((tl, nvp, LANES), lambda t: (t, 0, 0)),
        compiler_params=pltpu.CompilerParams(
            dimension_semantics=("parallel",), vmem_limit_bytes=VMEM_LIMIT),
        name="to_chains_v",
    )(z)
    return a, v


def _from_chains_kernel(y_ref, o_ref, zs, *, vq, nvp):
    B = o_ref.shape[0]
    C = B * RWKV_HEADS
    tl = o_ref.shape[1]
    for p in range(nvp):
        zs[p] = (y_ref[0, :, p, :] + y_ref[1, :, p, :]).T
    for b in range(B):
        rows = [zs[:, pl.ds(q * C + b * RWKV_HEADS, RWKV_HEADS), :] for q in range(vq)]
        slab = jnp.concatenate(rows, axis=0) if vq > 1 else rows[0]
        o_ref[b] = slab.reshape(D_RWKV, tl).T


def _from_chains_call(y, B, vq, nvp):
    T = y.shape[1]
    tl = min(RELAYOUT_T, T)
    return pl.pallas_call(
        functools.partial(_from_chains_kernel, vq=vq, nvp=nvp),
        out_shape=jax.ShapeDtypeStruct((B, T, D_RWKV), F32),
        grid=(T // tl,),
        in_specs=[pl.BlockSpec((2, tl, nvp, LANES), lambda t: (0, t, 0, 0))],
        out_specs=pl.BlockSpec((B, tl, D_RWKV), lambda t: (0, t, 0)),
        scratch_shapes=[pltpu.VMEM((nvp, LANES, tl), F32)],
        compiler_params=pltpu.CompilerParams(
            dimension_semantics=("parallel",), vmem_limit_bytes=VMEM_LIMIT),
        name="from_chains",
    )(y)


def _hgrn_direction(q, kf, val, lf, st_ref, tri, off_mask, sel, ones, rev, sub):
    C = q.shape[0]
    nb = C // sub
    l_hi, l_mid, l_lo = _split3(lf)
    G = _dot(tri, l_hi) + _dot(tri, l_mid) + _dot(tri, l_lo)
    last = 0 if rev else C - 1
    g_tot = G[last:last + 1, :]
    outs = []
    for h in range(HGRN_HEADS):
        sl = slice(h * HGRN_HEAD, (h + 1) * HGRN_HEAD)
        Gh, qh, kh, vh = G[:, sl], q[:, sl], kf[:, sl], val[:, sl]
        st = st_ref[h]
        o = _dot_nt((qh * jnp.exp(Gh)).astype(BF16), st.astype(BF16))
        rows = []
        for i in range(nb):
            if (i == nb - 1) if rev else (i == 0):
                rows.append(jnp.zeros((sub, C), F32))
                continue
            ref_row = (i + 1) * sub if rev else i * sub - 1
            ref = Gh[ref_row:ref_row + 1, :]
            qi = qh[i * sub:(i + 1) * sub] * jnp.exp(Gh[i * sub:(i + 1) * sub] - ref)
            ki = kh * jnp.exp(jnp.minimum(ref - Gh, 0.0))
            rows.append(_dot_nt(qi.astype(BF16), ki.astype(BF16)))
        a_off = jnp.concatenate(rows, axis=0) * off_mask
        o = o + _dot(a_off.astype(BF16), vh.astype(BF16))
        parts = []
        for t in range(C):
            b0 = (t // sub) * sub
            rel = jnp.exp(jnp.minimum(Gh[t:t + 1, :] - Gh[b0:b0 + sub, :], 0.0))
            parts.append(rel * kh[b0:b0 + sub, :] * qh[t:t + 1, :])
        ab = _dot(jnp.concatenate(parts, axis=0).astype(BF16), ones)
        wv = jnp.concatenate(
            [ab[t * sub:(t + 1) * sub] * vh[(t // sub) * sub:(t // sub + 1) * sub] for t in range(C)], axis=0)
        o = o + _dot(sel, wv.astype(BF16))
        outs.append(o)
        kdec = kh * jnp.exp(g_tot[:, sl] - Gh)
        st_ref[h] = st * jnp.exp(g_tot[:, sl]) + _dot_tn(vh.astype(BF16), kdec.astype(BF16))
    return jnp.concatenate(outs, axis=-1)


def _hgrn_kernel(qf_ref, vf_ref, lff_ref, kff_ref, qb_ref, vb_ref, lfb_ref, kfb_ref,
                 trif_ref, trib_ref, offf_ref, offb_ref, self_ref, selb_ref, ones_ref,
                 of_ref, ob_ref, stf_ref, stb_ref, *, chunk, sub):
    i = pl.program_id(1)
    n_c = qf_ref.shape[1] // chunk

    @pl.when(i == 0)
    def _():
        stf_ref[...] = jnp.zeros_like(stf_ref)
        stb_ref[...] = jnp.zeros_like(stb_ref)

    def body(c, carry):
        cf = pl.multiple_of(c * chunk, chunk)
        cb = pl.multiple_of((n_c - 1 - c) * chunk, chunk)
        of_ref[0, pl.ds(cf, chunk), :] = _hgrn_direction(
            qf_ref[0, pl.ds(cf, chunk), :], kff_ref[0, pl.ds(cf, chunk), :], vf_ref[0, pl.ds(cf, chunk), :],
            lff_ref[0, pl.ds(cf, chunk), :], stf_ref, trif_ref[...], offf_ref[...], self_ref[...], ones_ref[...],
            rev=False, sub=sub)
        ob_ref[0, pl.ds(cb, chunk), :] = _hgrn_direction(
            qb_ref[0, pl.ds(cb, chunk), :], kfb_ref[0, pl.ds(cb, chunk), :], vb_ref[0, pl.ds(cb, chunk), :],
            lfb_ref[0, pl.ds(cb, chunk), :], stb_ref, trib_ref[...], offb_ref[...], selb_ref[...], ones_ref[...],
            rev=True, sub=sub)
        return carry

    lax.fori_loop(0, n_c, body, 0)


def _hgrn_consts(chunk, sub):
    t = jnp.arange(chunk)
    tri_f = (t[None, :] <= t[:, None])
    blk = t // sub
    off_f = blk[None, :] < blk[:, None]
    col = jnp.arange(chunk * sub)
    same_t = (col // sub)[None, :] == t[:, None]
    j = (col % sub)[None, :]
    sel_f = same_t & (j <= (t % sub)[:, None])
    sel_b = same_t & (j >= (t % sub)[:, None])
    return [tri_f.astype(BF16), tri_f.T.astype(BF16), off_f.astype(F32), off_f.T.astype(F32),
            sel_f.astype(BF16), sel_b.astype(BF16), jnp.ones((HGRN_HEAD, HGRN_HEAD), BF16)]


def _hgrn_call(q, iv, lf, kf):
    B, T, _ = q.shape
    tile = min(HGRN_TILE, T)
    chunk = min(HGRN_CHUNK, tile)
    sub = min(HGRN_SUB, chunk)
    n_i = T // tile
    consts = _hgrn_consts(chunk, sub)
    fwd = pl.BlockSpec((1, tile, D_HGRN), lambda b, i: (b, i, 0))
    bwd = pl.BlockSpec((1, tile, D_HGRN), lambda b, i: (b, n_i - 1 - i, 0))
    fwd2 = pl.BlockSpec((None, 1, tile, D_HGRN), lambda b, i: (0, b, i, 0))
    bwd2 = pl.BlockSpec((None, 1, tile, D_HGRN), lambda b, i: (1, b, n_i - 1 - i, 0))
    full = [pl.BlockSpec(a.shape, lambda b, i: (0, 0)) for a in consts]
    out = jax.ShapeDtypeStruct((B, T, D_HGRN), F32)
    state = pltpu.VMEM((HGRN_HEADS, HGRN_HEAD, HGRN_HEAD), F32)
    return pl.pallas_call(
        functools.partial(_hgrn_kernel, chunk=chunk, sub=sub),
        out_shape=[out, out],
        grid=(B, n_i),
        in_specs=[fwd, fwd, fwd2, fwd2, bwd, bwd, bwd2, bwd2] + full,
        out_specs=[fwd, bwd],
        scratch_shapes=[state, state],
        compiler_params=pltpu.CompilerParams(
            dimension_semantics=("parallel", "arbitrary"), vmem_limit_bytes=VMEM_LIMIT),
        name="hgrn",
    )(q, iv, lf, kf, q, iv, lf, kf, *consts)


def _mix_kernel(x_ref, ya_ref, g_ref, bv_ref, of_ref, ob_ref, sg_ref, lnw_ref, lnb_ref, gnw_ref,
                hd64_ref, hd128_ref, wo_ref, n2_ref, rwh_ref, rwl_ref, rb_ref,
                x1_o, h2_o, lg_o):
    hd64 = hd64_ref[...]
    y = ya_ref[0]
    inv = 1.0 / RWKV_HEAD
    mean = _dot_lhs2(y, hd64) * inv
    yc = y - mean
    var = _dot_lhs2(yc * yc, hd64) * inv
    ya = (yc * lax.rsqrt(var + LNX_EPS)) * lnw_ref[...] + lnb_ref[...]
    ya = (ya + bv_ref[0]) * g_ref[0]
    o = of_ref[0] + ob_ref[0]
    ms = _dot_lhs2(o * o, hd128_ref[...]) * (1.0 / HGRN_HEAD)
    yb = o * lax.rsqrt(ms + HGRN_NORM_EPS) * gnw_ref[...] * sg_ref[0]
    mixed = jnp.concatenate([ya, yb], axis=-1).astype(BF16)
    x1 = x_ref[0] + _dot(mixed, wo_ref[...])
    x1_o[0] = x1
    ms2 = jnp.mean(x1 * x1, axis=-1, keepdims=True)
    h2 = x1 * lax.rsqrt(ms2 + NORM_EPS) * n2_ref[...]
    h2_o[0] = h2
    h_hi, h_lo = _split2(h2)
    w_hi = rwh_ref[...]
    lg_o[0] = _dot(h_hi, w_hi) + _dot(h_lo, w_hi) + _dot(h_hi, rwl_ref[...]) + rb_ref[...]


def _mix_call(x, ya, g, bv, of, ob, sg, p):
    B, T, D = x.shape
    rows = min(PROJ_ROWS, T)

    def full(a):
        nd = a.ndim
        return pl.BlockSpec(a.shape, lambda b, i, _n=nd: (0,) * _n)

    def tile(w):
        return pl.BlockSpec((1, rows, w), lambda b, i: (b, i, 0))

    consts = [p["lnx_w"], p["lnx_b"], p["gnorm_w"], p["hd64"], p["hd128"], p["w_out"], p["norm2_w"],
              p["rw_hi"], p["rw_lo"], p["rb"]]
    return pl.pallas_call(
        _mix_kernel,
        out_shape=[jax.ShapeDtypeStruct((B, T, D), F32), jax.ShapeDtypeStruct((B, T, D), F32),
                   jax.ShapeDtypeStruct((B, T, ROUTER_PAD), F32)],
        grid=(B, T // rows),
        in_specs=[tile(D)] + [tile(D_RWKV)] * 6 + [full(a) for a in consts],
        out_specs=[tile(D), tile(D), tile(ROUTER_PAD)],
        compiler_params=pltpu.CompilerParams(
            dimension_semantics=("parallel", "parallel"), vmem_limit_bytes=VMEM_LIMIT),
        name="mix",
    )(x, ya, g, bv, of, ob, sg, *consts)


def _moe_kernel(be_ref, nu_ref, tok0_ref, tokn_ref, dst_ref, gt_ref, h_hbm, wg_ref, wu_ref, wd_ref, y_hbm,
                xbuf, ybuf, gsem, ssem):
    del be_ref
    i = pl.program_id(0)
    nb_used = nu_ref[0]
    slot = lax.rem(i, 2)

    def gather_copy(tok, r, s):
        return pltpu.make_async_copy(h_hbm.at[pl.ds(tok, 1)], xbuf.at[s, pl.ds(r, 1)], gsem.at[s])

    def scatter_copy(dst, r, s):
        return pltpu.make_async_copy(ybuf.at[s, pl.ds(r, 1)], y_hbm.at[pl.ds(dst, 1)], ssem.at[s])

    def start_gather(tok_ref, s):
        def body(r, c):
            gather_copy(tok_ref[0, 0, r], r, s).start()
            return c
        lax.fori_loop(0, MOE_ROWS, body, 0, unroll=8)

    def start_scatter(s):
        def body(r, c):
            scatter_copy(dst_ref[0, 0, r], r, s).start()
            return c
        lax.fori_loop(0, MOE_ROWS, body, 0, unroll=8)

    def wait_gather(s):
        pltpu.make_async_copy(h_hbm.at[pl.ds(0, MOE_ROWS)], xbuf.at[s], gsem.at[s]).wait()

    def wait_scatter(s):
        pltpu.make_async_copy(ybuf.at[s], y_hbm.at[pl.ds(0, MOE_ROWS)], ssem.at[s]).wait()

    @pl.when(i == 0)
    def _():
        n_res = y_hbm.shape[0] - 2 * MOE_ROWS
        ybuf[...] = jnp.zeros_like(ybuf)
        for s in range(2):
            spare = pltpu.make_async_copy(ybuf.at[s], y_hbm.at[pl.ds(n_res + s * MOE_ROWS, MOE_ROWS)], ssem.at[s])
            spare.start()
            spare.wait()

    @pl.when(jnp.logical_and(i == 0, nb_used > 0))
    def _():
        start_gather(tok0_ref, 0)

    @pl.when(i + 1 < nb_used)
    def _():
        start_gather(tokn_ref, 1 - slot)

    @pl.when(i < nb_used)
    def _():
        wait_gather(slot)

        @pl.when(i >= 2)
        def _():
            wait_scatter(slot)

        x = xbuf[slot].astype(BF16)
        a = _dot(x, wg_ref[0])
        u = _dot(x, wu_ref[0])
        hid = (a * _sigmoid(a)) * u
        ybuf[slot] = _dot(hid.astype(BF16), wd_ref[0]) * gt_ref[...]
        start_scatter(slot)

        @pl.when(i == nb_used - 1)
        def _():
            wait_scatter(slot)

            @pl.when(i >= 1)
            def _():
                wait_scatter(1 - slot)


def _moe_call(block_e, nb_used, src_tok, dst_row, gate_slot, h2, p):
    N, D = h2.shape
    P = src_tok.shape[0]
    n_blocks = P // MOE_ROWS
    smem_blk = lambda f: pl.BlockSpec((1, 1, MOE_ROWS), f, memory_space=pltpu.SMEM)
    grid_spec = pltpu.PrefetchScalarGridSpec(
        num_scalar_prefetch=2,
        grid=(n_blocks,),
        in_specs=[
            smem_blk(lambda i, be, nu: (0, 0, 0)),
            smem_blk(lambda i, be, nu: (jnp.minimum(i + 1, n_blocks - 1), 0, 0)),
            smem_blk(lambda i, be, nu: (i, 0, 0)),
            pl.BlockSpec((MOE_ROWS, 1), lambda i, be, nu: (i, 0)),
            pl.BlockSpec(memory_space=pl.ANY),
            pl.BlockSpec((1, D, D_EXPERT), lambda i, be, nu: (be[i], 0, 0)),
            pl.BlockSpec((1, D, D_EXPERT), lambda i, be, nu: (be[i], 0, 0)),
            pl.BlockSpec((1, D_EXPERT, D), lambda i, be, nu: (be[i], 0, 0)),
        ],
        out_specs=pl.BlockSpec(memory_space=pl.ANY),
        scratch_shapes=[pltpu.VMEM((2, MOE_ROWS, D), F32), pltpu.VMEM((2, MOE_ROWS, D), F32),
                        pltpu.SemaphoreType.DMA((2,)), pltpu.SemaphoreType.DMA((2,))],
    )
    tok3 = src_tok.reshape(n_blocks, 1, MOE_ROWS)
    return pl.pallas_call(
        _moe_kernel,
        out_shape=jax.ShapeDtypeStruct((2 * N + 2 * MOE_ROWS, D), F32),
        grid_spec=grid_spec,
        compiler_params=pltpu.CompilerParams(
            dimension_semantics=("arbitrary",), vmem_limit_bytes=VMEM_LIMIT),
        name="moe",
    )(block_e, nb_used, tok3, tok3, dst_row.reshape(n_blocks, 1, MOE_ROWS), gate_slot, h2,
      p["moe_wg"], p["moe_wu"], p["moe_wd"])


def _route_and_moe(h2, logits, p):
    N, D = h2.shape
    glog = logits[:, :N_GROUPS]
    elog = logits[:, N_GROUPS:N_GROUPS + N_EXPERTS].reshape(N, N_GROUPS, EXPERTS_PER_GROUP)
    gprob = jax.nn.softmax(glog, axis=-1)
    grp = jnp.argmax(glog, axis=-1)
    p_grp = jnp.take_along_axis(gprob, grp[:, None], axis=-1)
    elog_g = jnp.take_along_axis(elog, grp[:, None, None], axis=1)[:, 0]
    top_v, top_i = lax.top_k(elog_g, 2)
    gate = (p_grp * jax.nn.softmax(top_v, axis=-1)).reshape(2 * N)
    eid = (grp[:, None] * EXPERTS_PER_GROUP + top_i).reshape(2 * N).astype(jnp.int32)

    M = 2 * N
    order = jnp.argsort(eid, stable=True).astype(jnp.int32)
    experts = jnp.arange(N_EXPERTS, dtype=jnp.int32)
    counts = jnp.sum((eid[:, None] == experts[None, :]).astype(jnp.int32), axis=0)
    starts = jnp.cumsum(counts) - counts
    padded = (counts + MOE_ROWS - 1) // MOE_ROWS * MOE_ROWS
    pad_ends = jnp.cumsum(padded)
    pad_starts = pad_ends - padded
    n_blocks = (M + N_EXPERTS * (MOE_ROWS - 1) + MOE_ROWS - 1) // MOE_ROWS
    P = n_blocks * MOE_ROWS
    block_e = jnp.minimum(
        jnp.sum(pad_ends[None, :] <= (jnp.arange(n_blocks, dtype=jnp.int32) * MOE_ROWS)[:, None], axis=1),
        N_EXPERTS - 1).astype(jnp.int32)
    slot = jnp.arange(P, dtype=jnp.int32)
    e_slot = jnp.repeat(block_e, MOE_ROWS)
    off = slot - pad_starts[e_slot]
    valid = (off >= 0) & (off < counts[e_slot])
    src = jnp.clip(starts[e_slot] + off, 0, M - 1)
    assign = order[src]
    gate_slot = jnp.where(valid, gate[assign], 0.0)[:, None]
    src_tok = jnp.where(valid, assign // 2, 0)
    spare = M + ((slot // MOE_ROWS) % 2) * MOE_ROWS + slot % MOE_ROWS
    dst_row = jnp.where(valid, assign, spare)
    nb_used = (pad_ends[-1] // MOE_ROWS).astype(jnp.int32).reshape(1)
    return _moe_call(block_e, nb_used, src_tok, dst_row, gate_slot, h2, p)


def _final_kernel(x_ref, m_ref, w_ref, o_ref):
    D = x_ref.shape[1]
    x = x_ref[...] + m_ref[:, 0:D] + m_ref[:, D:2 * D]
    ms = jnp.mean(x * x, axis=-1, keepdims=True)
    o_ref[...] = x * lax.rsqrt(ms + NORM_EPS) * w_ref[...]


def _final_call(x1, moe2, w):
    N, D = x1.shape
    rows = min(PROJ_ROWS * 2, N)
    tile = pl.BlockSpec((rows, D), lambda i: (i, 0))
    pair = pl.BlockSpec((rows, 2 * D), lambda i: (i, 0))
    moe = moe2.reshape(moe2.shape[0] // 2, 2 * D)
    return pl.pallas_call(
        _final_kernel,
        out_shape=jax.ShapeDtypeStruct((N, D), F32),
        grid=(N // rows,),
        in_specs=[tile, pair, pl.BlockSpec((1, D), lambda i: (0, 0))],
        out_specs=tile,
        compiler_params=pltpu.CompilerParams(
            dimension_semantics=("parallel",), vmem_limit_bytes=VMEM_LIMIT),
        name="final",
    )(x1, moe, w)


def _block_diag_ones(width, head):
    idx = jnp.arange(width) // head
    return (idx[:, None] == idx[None, :]).astype(BF16)


def _prepare(norm1_w, w_in, mu_shift, rwkv_w0, rwkv_w2, rwkv_a0, rwkv_a2, rwkv_g2, rwkv_k_k, rwkv_k_a,
             rwkv_r_k, rwkv_lnx_w, rwkv_lnx_b, hgrn_lb, hgrn_gnorm_w, w_out, norm2_w, router_group_w,
             router_group_b, router_expert_w, router_expert_b, moe_w_gate, moe_w_up, moe_w_down, final_norm_w):
    row = lambda a: a.reshape(1, -1).astype(F32)
    lb_all = jnp.cumsum(jax.nn.softmax(hgrn_lb.astype(F32), axis=1), axis=1)
    rw = jnp.concatenate([router_group_w[0], router_expert_w[0]], axis=1)
    rw = jnp.pad(rw, ((0, 0), (0, ROUTER_PAD - rw.shape[1])))
    rw_hi = rw.astype(BF16)
    rw_lo = (rw - rw_hi.astype(F32)).astype(BF16)
    rb = jnp.pad(jnp.concatenate([router_group_b[0], router_expert_b[0]]), (0, ROUTER_PAD - N_GROUPS - N_EXPERTS))
    perm = jnp.arange(D_RWKV).reshape(RWKV_HEADS, RWKV_HEAD).T.reshape(-1)
    cols = jnp.concatenate([perm, C_K + perm, C_V + perm, jnp.arange(C_WD, C_RWKV_END)])
    col = lambda a: jnp.broadcast_to(a.astype(F32)[..., None], a.shape + (LANES,))
    head = jnp.arange(D_RWKV) % RWKV_HEADS
    return {
        "norm1_w": row(norm1_w[0]),
        "w_r": w_in[0][:, :C_RWKV_END].T[cols].astype(BF16),
        "w_h": w_in[0][:, C_RWKV_END:].astype(BF16),
        "mu": col(mu_shift[0][cols]),
        "w0": col(rwkv_w0[0][:, perm]), "w2": jnp.swapaxes(rwkv_w2[0], 1, 2)[:, perm].astype(BF16),
        "a0": col(rwkv_a0[0][:, perm]), "a2": jnp.swapaxes(rwkv_a2[0], 1, 2)[:, perm].astype(BF16),
        "g2": rwkv_g2[0][:, perm].astype(BF16),
        "k_k": col(rwkv_k_k[0][perm]), "k_a": col(rwkv_k_a[0][perm]),
        "r_k": col(rwkv_r_k[0].reshape(-1)[perm]),
        "lnx_w": row(rwkv_lnx_w[0][perm]), "lnx_b": row(rwkv_lnx_b[0][perm]),
        "lb": lb_all[:, 0],
        "gnorm_w": row(jnp.tile(hgrn_gnorm_w[0], HGRN_HEADS)),
        "hd64": (head[:, None] == head[None, :]).astype(BF16),
        "hd128": _block_diag_ones(D_HGRN, HGRN_HEAD),
        "w_out": jnp.concatenate([w_out[0][:D_RWKV][perm], w_out[0][D_RWKV:]]).astype(BF16),
        "norm2_w": row(norm2_w[0]),
        "rw_hi": rw_hi, "rw_lo": rw_lo, "rb": row(rb),
        "moe_wg": moe_w_gate[0].astype(BF16), "moe_wu": moe_w_up[0].astype(BF16),
        "moe_wd": moe_w_down[0].astype(BF16),
        "final_w": row(final_norm_w),
    }


def _forward(x, p):
    B, T, D = x.shape
    z, g, bv, q, lf, kf, iv, sg = _proj_call(x, p)

    chains = B * RWKV_HEADS
    vq = LANES // chains
    nv = RWKV_HEAD // (vq * SUBLANES)
    a, v = _to_chains_call(z, vq, nv * SUBLANES)
    ya = _from_chains_call(_rwkv_call(a, v, nv), B, vq, nv * SUBLANES)

    of, ob = _hgrn_call(q, iv, lf, kf)
    x1, h2, logits = _mix_call(x, ya, g, bv, of, ob, sg, p)
    N = B * T
    moe = _route_and_moe(h2.reshape(N, D), logits.reshape(N, ROUTER_PAD), p)
    return _final_call(x1.reshape(N, D), moe, p["final_w"]).reshape(B, T, D)


def kernel(x_prompt, x_sample, norm1_w, w_in, mu_shift, rwkv_w0, rwkv_w2, rwkv_a0, rwkv_a2, rwkv_g2, rwkv_k_k, rwkv_k_a, rwkv_r_k, rwkv_lnx_w, rwkv_lnx_b, hgrn_lb, hgrn_gnorm_w, w_out, norm2_w, router_group_w, router_group_b, router_expert_w, router_expert_b, moe_w_gate, moe_w_up, moe_w_down, final_norm_w):
    p = _prepare(norm1_w, w_in, mu_shift, rwkv_w0, rwkv_w2, rwkv_a0, rwkv_a2, rwkv_g2, rwkv_k_k, rwkv_k_a,
                 rwkv_r_k, rwkv_lnx_w, rwkv_lnx_b, hgrn_lb, hgrn_gnorm_w, w_out, norm2_w, router_group_w,
                 router_group_b, router_expert_w, router_expert_b, moe_w_gate, moe_w_up, moe_w_down,
                 final_norm_w)
    return (_forward(x_prompt, p), _forward(x_sample, p))
```

```python
import functools

import jax
import jax.numpy as jnp
from jax import lax
from jax.experimental import pallas as pl
from jax.experimental.pallas import tpu as pltpu

F32 = jnp.float32
BF16 = jnp.bfloat16

D_MODEL = 1024
D_RWKV = 512
RWKV_HEAD = 64
RWKV_HEADS = 8
D_HGRN = 512
HGRN_HEAD = 128
HGRN_HEADS = 4
DECAY_LORA = 64
AAA_LORA = 64
GATE_LORA = 128
N_GROUPS = 4
EXPERTS_PER_GROUP = 8
N_EXPERTS = 32
D_EXPERT = 512
NORM_EPS = 1e-6
HGRN_NORM_EPS = 1e-5
LNX_EPS = RWKV_HEAD * 1e-5

C_K = 512
C_V = 1024
C_WD = 1536
C_AD = 1664
C_GD = 1792
C_RWKV_END = 1920
D_IN = 4480
HG_Q = 0
HG_F = 512
HG_I = 1536
HG_G = 2048
D_HG_IN = 2560

SUBLANES = 8
LANES = 128
VMEM_LIMIT = 56 * 1024 * 1024

PROJ_ROWS = 256
HALO = 8
SCAN_STEPS = 32
HGRN_CHUNK = 64
HGRN_SUB = 8
LOG2E = 1.4426950408889634
HGRN_TILE = 256
MOE_ROWS = 256
ROUTER_PAD = 128
RELAYOUT_T = 128

Z_R, Z_KK, Z_W, Z_K, Z_B, Z_V = 0, 1, 2, 4, 6, 8
Z_COUNT = 9


def _dot(a, b):
    return jnp.dot(a, b, preferred_element_type=F32)


def _dot_nt(a, b):
    return lax.dot_general(a, b, (((1,), (1,)), ((), ())), preferred_element_type=F32)


def _dot_tn(a, b):
    return lax.dot_general(a, b, (((0,), (0,)), ((), ())), preferred_element_type=F32)


def _split2(a):
    hi = a.astype(BF16)
    lo = (a - hi.astype(F32)).astype(BF16)
    return hi, lo


def _split3(a):
    hi = a.astype(BF16)
    r1 = a - hi.astype(F32)
    mid = r1.astype(BF16)
    lo = (r1 - mid.astype(F32)).astype(BF16)
    return hi, mid, lo


def _dot_lhs2(a, b_bf16):
    hi, lo = _split2(a)
    return _dot(hi, b_bf16) + _dot(lo, b_bf16)


def _sigmoid(x):
    return 1.0 / (1.0 + jnp.exp(-x))


def _softplus(x):
    return jnp.maximum(x, 0.0) + jnp.log(1.0 + jnp.exp(-jnp.abs(x)))


def _lanes(c, n):
    if n == LANES:
        return c
    if n < LANES:
        return c[:, :n]
    return jnp.concatenate([c] * (n // LANES), axis=1)


def _dot_rhs2(a_bf16, b):
    hi, lo = _split2(b)
    return _dot(a_bf16, hi) + _dot(a_bf16, lo)


def _proj_kernel(x_ref, xp_ref, xn_ref, n1_ref, wr_ref, wh_ref, mu_ref, w0_ref, w2_ref, a0_ref, a2_ref,
                 g2_ref, kk_ref, ka_ref, rk_ref, lb_ref, hd_ref,
                 z_o, g_o, bv_o, q_o, lf_o, kf_o, iv_o, sg_o):
    i = pl.program_id(1)
    n_i = pl.num_programs(1)
    rows = x_ref.shape[1]

    def norm(xf):
        ms = jnp.mean(xf * xf, axis=-1, keepdims=True)
        return (xf * lax.rsqrt(ms + NORM_EPS) * n1_ref[...]).astype(BF16)

    h_c = norm(x_ref[0])
    h_h = norm(jnp.concatenate([xp_ref[0] * (i > 0).astype(F32), xn_ref[0] * (i < n_i - 1).astype(F32)], axis=0))

    wr = wr_ref[...]
    u_c = _dot_nt(wr, h_c)
    u_h = _dot_nt(wr, h_h)
    lane = lax.broadcasted_iota(jnp.int32, (1, rows), 1)
    u_prev = jnp.where(lane == 0, u_h[:, HALO - 1:HALO], pltpu.roll(u_c, 1, 1))
    u_next = jnp.where(lane == rows - 1, u_h[:, HALO:HALO + 1], pltpu.roll(u_c, rows - 1, 1))
    us = u_c + (0.5 * (u_prev + u_next) - u_c) * _lanes(mu_ref[...], rows)

    def put(j, a):
        z_o[j] = a.reshape(RWKV_HEAD, SUBLANES, rows)

    r = us[0:C_K]
    k = us[C_K:C_V]
    v = us[C_V:C_WD]
    hd = hd_ref[...]
    kk = k * _lanes(kk_ref[...], rows)
    kn = kk / jnp.maximum(jnp.sqrt(_dot_rhs2(hd, kk * kk)), 1e-12)
    gd = us[C_GD:C_RWKV_END]
    g_o[0] = _dot_tn(_sigmoid(gd).astype(BF16), g2_ref[...])
    put(Z_R, r)
    put(Z_KK, kn)
    put(Z_V, v)
    kd_sum = jnp.zeros_like(k)
    for d in range(2):
        wd = us[C_WD + d * DECAY_LORA:C_WD + (d + 1) * DECAY_LORA]
        ad = us[C_AD + d * AAA_LORA:C_AD + (d + 1) * AAA_LORA]
        zw = _lanes(w0_ref[d], rows) + _dot(w2_ref[d], jnp.tanh(wd).astype(BF16))
        w_log = -_softplus(-zw) - 0.5
        put(Z_W + d, jnp.exp(-jnp.exp(w_log)))
        a = _sigmoid(_lanes(a0_ref[d], rows) + _dot(a2_ref[d], ad.astype(BF16)))
        kd = k * (1.0 + (a - 1.0) * _lanes(ka_ref[...], rows))
        put(Z_K + d, kd)
        put(Z_B + d, kn * a)
        kd_sum = kd_sum + kd
    bonus = _dot_rhs2(hd, r * kd_sum * _lanes(rk_ref[...], rows))
    bv_o[0] = (bonus * v).T

    uh = _dot(h_c, wh_ref[...])
    uq = uh[:, HG_Q:HG_F]
    q_o[0] = uq * _sigmoid(uq)
    iv_o[0] = uh[:, HG_I:HG_G]
    ug = uh[:, HG_G:D_HG_IN]
    sg_o[0] = ug * _sigmoid(ug)
    for d in range(2):
        fr = uh[:, HG_F + d * D_HGRN:HG_F + (d + 1) * D_HGRN]
        lb = lb_ref[d:d + 1, :]
        f = lb + (1.0 - lb) * _sigmoid(fr)
        lf_o[d, 0] = jnp.log(f)
        kf_o[d, 0] = 1.0 - f


def _proj_call(x, p):
    B, T, D = x.shape
    rows = min(PROJ_ROWS, T)
    n_i = T // rows
    rb = rows // HALO

    def full(a):
        nd = a.ndim
        return pl.BlockSpec(a.shape, lambda b, i, _n=nd: (0,) * _n)

    tile = pl.BlockSpec((1, rows, D), lambda b, i: (b, i, 0))
    prev = pl.BlockSpec((1, HALO, D), lambda b, i: (b, jnp.maximum(i * rb - 1, 0), 0))
    nxt = pl.BlockSpec((1, HALO, D), lambda b, i: (b, jnp.minimum((i + 1) * rb, T // HALO - 1), 0))
    consts = [p["norm1_w"], p["w_r"], p["w_h"], p["mu"], p["w0"], p["w2"], p["a0"], p["a2"], p["g2"],
              p["k_k"], p["k_a"], p["r_k"], p["lb"], p["hd64"]]
    one = jax.ShapeDtypeStruct((B, T, D_RWKV), F32)
    two = jax.ShapeDtypeStruct((2, B, T, D_RWKV), F32)
    one_spec = pl.BlockSpec((1, rows, D_RWKV), lambda b, i: (b, i, 0))
    two_spec = pl.BlockSpec((2, 1, rows, D_RWKV), lambda b, i: (0, b, i, 0))
    z = jax.ShapeDtypeStruct((Z_COUNT, RWKV_HEAD, B * RWKV_HEADS, T), F32)
    z_spec = pl.BlockSpec((Z_COUNT, RWKV_HEAD, RWKV_HEADS, rows), lambda b, i: (0, 0, b, i))
    kinds = [z, one, one, one, two, two, one, one]
    return pl.pallas_call(
        _proj_kernel,
        out_shape=kinds,
        grid=(B, n_i),
        in_specs=[tile, prev, nxt] + [full(a) for a in consts],
        out_specs=[z_spec if s is z else (one_spec if s is one else two_spec) for s in kinds],
        compiler_params=pltpu.CompilerParams(
            dimension_semantics=("parallel", "arbitrary"), vmem_limit_bytes=VMEM_LIMIT),
        name="proj",
    )(x, x, x, *consts)


def _rwkv_kernel(r_ref, kk_ref, v_ref, w_ref, k_ref, b_ref, y_ref, s_ref, *, nv, steps):
    d = pl.program_id(0)
    i = pl.program_id(1)
    K = RWKV_HEAD
    n_par = max(1, SUBLANES // nv)

    @pl.when(i == 0)
    def _():
        s_ref[...] = jnp.zeros_like(s_ref)

    def row(ref, tt, k):
        return jnp.broadcast_to(ref[tt[0], k, pl.ds(tt[1], 1), :], (SUBLANES, LANES))

    def tree_sum(parts):
        parts = [x for x in parts if x is not None]
        while len(parts) > 1:
            parts = [parts[j] + parts[j + 1] if j + 1 < len(parts) else parts[j]
                     for j in range(0, len(parts), 2)]
        return parts[0]

    def step(j, carry):
        t = jnp.where(d == 0, j, steps - 1 - j)
        tt = (t // SUBLANES, t % SUBLANES)
        acc = [[None] * n_par for _ in range(nv)]
        for k in range(K):
            kk_row = row(kk_ref, tt, k)
            for vg in range(nv):
                term = s_ref[vg * K + k] * kk_row
                slot = k % n_par
                acc[vg][slot] = term if acc[vg][slot] is None else acc[vg][slot] + term
        sa = [-tree_sum(acc[vg]) for vg in range(nv)]
        val = [v_ref[t, pl.ds(vg * SUBLANES, SUBLANES), :] for vg in range(nv)]
        yacc = [[None] * n_par for _ in range(nv)]
        for k in range(K):
            w_row = row(w_ref, tt, k)
            b_row = row(b_ref, tt, k)
            k_row = row(k_ref, tt, k)
            r_row = row(r_ref, tt, k)
            for vg in range(nv):
                s_new = s_ref[vg * K + k] * w_row + sa[vg] * b_row + val[vg] * k_row
                s_ref[vg * K + k] = s_new
                term = s_new * r_row
                slot = k % n_par
                yacc[vg][slot] = term if yacc[vg][slot] is None else yacc[vg][slot] + term
        for vg in range(nv):
            y_ref[t, pl.ds(vg * SUBLANES, SUBLANES), :] = tree_sum(yacc[vg])
        return carry

    lax.fori_loop(0, steps, step, 0)


def _rwkv_call(a, v, nv):
    T = a.shape[1] * SUBLANES
    steps = min(SCAN_STEPS, T)
    n_t = T // steps
    nvp = nv * SUBLANES

    def tblk(d, i):
        return jnp.where(d == 0, i, n_t - 1 - i)

    kblock = (None, steps // SUBLANES, RWKV_HEAD, SUBLANES, LANES)

    def shared(j):
        return pl.BlockSpec(kblock, lambda d, i: (j, tblk(d, i), 0, 0, 0))

    def perdir(j):
        return pl.BlockSpec(kblock, lambda d, i: (j + d, tblk(d, i), 0, 0, 0))

    vspec = pl.BlockSpec((steps, nvp, LANES), lambda d, i: (tblk(d, i), 0, 0))
    yspec = pl.BlockSpec((None, steps, nvp, LANES), lambda d, i: (d, tblk(d, i), 0, 0))
    return pl.pallas_call(
        functools.partial(_rwkv_kernel, nv=nv, steps=steps),
        out_shape=jax.ShapeDtypeStruct((2, T, nvp, LANES), F32),
        grid=(2, n_t),
        in_specs=[shared(Z_R), shared(Z_KK), vspec, perdir(Z_W), perdir(Z_K), perdir(Z_B)],
        out_specs=yspec,
        scratch_shapes=[pltpu.VMEM((nv * RWKV_HEAD, SUBLANES, LANES), F32)],
        compiler_params=pltpu.CompilerParams(
            dimension_semantics=("parallel", "arbitrary"), vmem_limit_bytes=VMEM_LIMIT),
        name="rwkv",
    )(a, a, v, a, a, a)


def _to_chains_k_kernel(z_ref, o_ref, *, vq):
    tl = z_ref.shape[2]
    for p in range(RWKV_HEAD):
        a = z_ref[p]
        a = (jnp.concatenate([a] * vq, axis=0) if vq > 1 else a).T
        o_ref[:, p] = a.reshape(tl // SUBLANES, SUBLANES, LANES)


def _to_chains_v_kernel(z_ref, o_ref, *, vq, nvp):
    for p in range(nvp):
        o_ref[:, p, :] = jnp.concatenate([z_ref[q * nvp + p] for q in range(vq)], axis=0).T


def _to_chains_call(z, vq, nvp):
    n_arr, planes, C, T = z.shape
    tl = min(RELAYOUT_T, T)
    a = pl.pallas_call(
        functools.partial(_to_chains_k_kernel, vq=vq),
        out_shape=jax.ShapeDtypeStruct((Z_V, T // SUBLANES, planes, SUBLANES, LANES), F32),
        grid=(Z_V, T // tl),
        in_specs=[pl.BlockSpec((None, planes, C, tl), lambda j, t: (j, 0, 0, t))],
        out_specs=pl.BlockSpec((None, tl // SUBLANES, planes, SUBLANES, LANES), lambda j, t: (j, t, 0, 0, 0)),
        compiler_params=pltpu.CompilerParams(
            dimension_semantics=("parallel", "parallel"), vmem_limit_bytes=VMEM_LIMIT),
        name="to_chains_k",
    )(z)
    v = pl.pallas_call(
        functools.partial(_to_chains_v_kernel, vq=vq, nvp=nvp),
        out_shape=jax.ShapeDtypeStruct((T, nvp, LANES), F32),
        grid=(T // tl,),
        in_specs=[pl.BlockSpec((None, planes, C, tl), lambda t: (Z_V, 0, 0, t))],
        out_specs=pl.BlockSpec((tl, nvp, LANES), lambda t: (t, 0, 0)),
        compiler_params=pltpu.CompilerParams(
            dimension_semantics=("parallel",), vmem_limit_bytes=VMEM_LIMIT),
        name="to_chains_v",
    )(z)
    return a, v


def _from_chains_kernel(y_ref, o_ref, zs, *, vq, nvp):
    B = o_ref.shape[0]
    C = B * RWKV_HEADS
    tl = o_ref.shape[1]
    for p in range(nvp):
        zs[p] = (y_ref[0, :, p, :] + y_ref[1, :, p, :]).T
    for b in range(B):
        rows = [zs[:, pl.ds(q * C + b * RWKV_HEADS, RWKV_HEADS), :] for q in range(vq)]
        slab = jnp.concatenate(rows, axis=0) if vq > 1 else rows[0]
        o_ref[b] = slab.reshape(D_RWKV, tl).T


def _from_chains_call(y, B, vq, nvp):
    T = y.shape[1]
    tl = min(RELAYOUT_T, T)
    return pl.pallas_call(
        functools.partial(_from_chains_kernel, vq=vq, nvp=nvp),
        out_shape=jax.ShapeDtypeStruct((B, T, D_RWKV), F32),
        grid=(T // tl,),
        in_specs=[pl.BlockSpec((2, tl, nvp, LANES), lambda t: (0, t, 0, 0))],
        out_specs=pl.BlockSpec((B, tl, D_RWKV), lambda t: (0, t, 0)),
        scratch_shapes=[pltpu.VMEM((nvp, LANES, tl), F32)],
        compiler_params=pltpu.CompilerParams(
            dimension_semantics=("parallel",), vmem_limit_bytes=VMEM_LIMIT),
        name="from_chains",
    )(y)


def _hgrn_direction(q, kf, val, lf, st_ref, tri, off_mask, sel, ones, rev, sub):
    C = q.shape[0]
    l_hi, l_mid, l_lo = _split3(lf)
    G = (_dot(tri, l_hi) + _dot(tri, l_mid) + _dot(tri, l_lo)) * LOG2E
    last = 0 if rev else C - 1
    g_tot = G[last:last + 1, :]
    outs = []
    for h in range(HGRN_HEADS):
        sl = slice(h * HGRN_HEAD, (h + 1) * HGRN_HEAD)
        Gh, qh, kh, vh = G[:, sl], q[:, sl], kf[:, sl], val[:, sl]
        st = st_ref[h]
        o = _dot_nt((qh * jnp.exp2(Gh)).astype(BF16), st.astype(BF16))
        a_off = None
        for lv, L in enumerate(_hgrn_levels(C, sub)):
            mid = L // 2 if rev else L // 2 - 1
            refs = [jnp.broadcast_to(Gh[m * L + mid:m * L + mid + 1, :], (L, HGRN_HEAD)) for m in range(C // L)]
            ref = jnp.concatenate(refs, axis=0) if len(refs) > 1 else refs[0]
            ql = qh * jnp.exp2(jnp.minimum(Gh - ref, 0.0))
            kl = kh * jnp.exp2(jnp.minimum(ref - Gh, 0.0))
            al = _dot_nt(ql.astype(BF16), kl.astype(BF16)) * off_mask[lv]
            a_off = al if a_off is None else a_off + al
        if a_off is not None:
            o = o + _dot(a_off.astype(BF16), vh.astype(BF16))
        parts = []
        for t in range(C):
            b0 = (t // sub) * sub
            rel = jnp.exp2(jnp.minimum(Gh[t:t + 1, :] - Gh[b0:b0 + sub, :], 0.0))
            parts.append(rel * kh[b0:b0 + sub, :] * qh[t:t + 1, :])
        ab = _dot(jnp.concatenate(parts, axis=0).astype(BF16), ones)
        wv = jnp.concatenate(
            [ab[t * sub:(t + 1) * sub] * vh[(t // sub) * sub:(t // sub + 1) * sub] for t in range(C)], axis=0)
        o = o + _dot(sel, wv.astype(BF16))
        outs.append(o)
        kdec = kh * jnp.exp2(g_tot[:, sl] - Gh)
        st_ref[h] = st * jnp.exp2(g_tot[:, sl]) + _dot_tn(vh.astype(BF16), kdec.astype(BF16))
    return jnp.concatenate(outs, axis=-1)


def _hgrn_kernel(qf_ref, vf_ref, lff_ref, kff_ref, qb_ref, vb_ref, lfb_ref, kfb_ref,
                 trif_ref, trib_ref, offf_ref, offb_ref, self_ref, selb_ref, ones_ref,
                 of_ref, ob_ref, stf_ref, stb_ref, *, chunk, sub):
    i = pl.program_id(1)
    n_c = qf_ref.shape[1] // chunk

    @pl.when(i == 0)
    def _():
        stf_ref[...] = jnp.zeros_like(stf_ref)
        stb_ref[...] = jnp.zeros_like(stb_ref)

    def body(c, carry):
        cf = pl.multiple_of(c * chunk, chunk)
        cb = pl.multiple_of((n_c - 1 - c) * chunk, chunk)
        of_ref[0, pl.ds(cf, chunk), :] = _hgrn_direction(
            qf_ref[0, pl.ds(cf, chunk), :], kff_ref[0, pl.ds(cf, chunk), :], vf_ref[0, pl.ds(cf, chunk), :],
            lff_ref[0, pl.ds(cf, chunk), :], stf_ref, trif_ref[...], offf_ref[...], self_ref[...], ones_ref[...],
            rev=False, sub=sub)
        ob_ref[0, pl.ds(cb, chunk), :] = _hgrn_direction(
            qb_ref[0, pl.ds(cb, chunk), :], kfb_ref[0, pl.ds(cb, chunk), :], vb_ref[0, pl.ds(cb, chunk), :],
            lfb_ref[0, pl.ds(cb, chunk), :], stb_ref, trib_ref[...], offb_ref[...], selb_ref[...], ones_ref[...],
            rev=True, sub=sub)
        return carry

    lax.fori_loop(0, n_c, body, 0)


def _hgrn_levels(chunk, sub):
    levels = []
    L = 2 * sub
    while L <= chunk:
        levels.append(L)
        L *= 2
    return levels


def _hgrn_consts(chunk, sub):
    t = jnp.arange(chunk)
    tri_f = (t[None, :] <= t[:, None])
    off_f = jnp.stack([(t[:, None] // L == t[None, :] // L) & (t[:, None] % L >= L // 2) & (t[None, :] % L < L // 2)
                       for L in _hgrn_levels(chunk, sub)] or [jnp.zeros((chunk, chunk), bool)])
    col = jnp.arange(chunk * sub)
    same_t = (col // sub)[None, :] == t[:, None]
    j = (col % sub)[None, :]
    sel_f = same_t & (j <= (t % sub)[:, None])
    sel_b = same_t & (j >= (t % sub)[:, None])
    return [tri_f.astype(BF16), tri_f.T.astype(BF16), off_f.astype(F32), jnp.swapaxes(off_f, 1, 2).astype(F32),
            sel_f.astype(BF16), sel_b.astype(BF16), jnp.ones((HGRN_HEAD, HGRN_HEAD), BF16)]


def _hgrn_call(q, iv, lf, kf):
    B, T, _ = q.shape
    tile = min(HGRN_TILE, T)
    chunk = min(HGRN_CHUNK, tile)
    sub = min(HGRN_SUB, chunk)
    n_i = T // tile
    consts = _hgrn_consts(chunk, sub)
    fwd = pl.BlockSpec((1, tile, D_HGRN), lambda b, i: (b, i, 0))
    bwd = pl.BlockSpec((1, tile, D_HGRN), lambda b, i: (b, n_i - 1 - i, 0))
    fwd2 = pl.BlockSpec((None, 1, tile, D_HGRN), lambda b, i: (0, b, i, 0))
    bwd2 = pl.BlockSpec((None, 1, tile, D_HGRN), lambda b, i: (1, b, n_i - 1 - i, 0))
    full = [pl.BlockSpec(a.shape, lambda b, i, _n=a.ndim: (0,) * _n) for a in consts]
    out = jax.ShapeDtypeStruct((B, T, D_HGRN), F32)
    state = pltpu.VMEM((HGRN_HEADS, HGRN_HEAD, HGRN_HEAD), F32)
    return pl.pallas_call(
        functools.partial(_hgrn_kernel, chunk=chunk, sub=sub),
        out_shape=[out, out],
        grid=(B, n_i),
        in_specs=[fwd, fwd, fwd2, fwd2, bwd, bwd, bwd2, bwd2] + full,
        out_specs=[fwd, bwd],
        scratch_shapes=[state, state],
        compiler_params=pltpu.CompilerParams(
            dimension_semantics=("parallel", "arbitrary"), vmem_limit_bytes=VMEM_LIMIT),
        name="hgrn",
    )(q, iv, lf, kf, q, iv, lf, kf, *consts)


def _mix_kernel(x_ref, ya_ref, g_ref, bv_ref, of_ref, ob_ref, sg_ref, lnw_ref, lnb_ref, gnw_ref,
                hd64_ref, hd128_ref, wo_ref, n2_ref, rwh_ref, rwl_ref, rb_ref,
                x1_o, h2_o, lg_o):
    hd64 = hd64_ref[...]
    y = ya_ref[0]
    inv = 1.0 / RWKV_HEAD
    mean = _dot_lhs2(y, hd64) * inv
    yc = y - mean
    var = _dot_lhs2(yc * yc, hd64) * inv
    ya = (yc * lax.rsqrt(var + LNX_EPS)) * lnw_ref[...] + lnb_ref[...]
    ya = (ya + bv_ref[0]) * g_ref[0]
    o = of_ref[0] + ob_ref[0]
    ms = _dot_lhs2(o * o, hd128_ref[...]) * (1.0 / HGRN_HEAD)
    yb = o * lax.rsqrt(ms + HGRN_NORM_EPS) * gnw_ref[...] * sg_ref[0]
    mixed = jnp.concatenate([ya, yb], axis=-1).astype(BF16)
    x1 = x_ref[0] + _dot(mixed, wo_ref[...])
    x1_o[0] = x1
    ms2 = jnp.mean(x1 * x1, axis=-1, keepdims=True)
    h2 = x1 * lax.rsqrt(ms2 + NORM_EPS) * n2_ref[...]
    h2_o[0] = h2
    h_hi, h_lo = _split2(h2)
    w_hi = rwh_ref[...]
    lg_o[0] = _dot(h_hi, w_hi) + _dot(h_lo, w_hi) + _dot(h_hi, rwl_ref[...]) + rb_ref[...]


def _mix_call(x, ya, g, bv, of, ob, sg, p):
    B, T, D = x.shape
    rows = min(PROJ_ROWS, T)

    def full(a):
        nd = a.ndim
        return pl.BlockSpec(a.shape, lambda b, i, _n=nd: (0,) * _n)

    def tile(w):
        return pl.BlockSpec((1, rows, w), lambda b, i: (b, i, 0))

    consts = [p["lnx_w"], p["lnx_b"], p["gnorm_w"], p["hd64"], p["hd128"], p["w_out"], p["norm2_w"],
              p["rw_hi"], p["rw_lo"], p["rb"]]
    return pl.pallas_call(
        _mix_kernel,
        out_shape=[jax.ShapeDtypeStruct((B, T, D), F32), jax.ShapeDtypeStruct((B, T, D), F32),
                   jax.ShapeDtypeStruct((B, T, ROUTER_PAD), F32)],
        grid=(B, T // rows),
        in_specs=[tile(D)] + [tile(D_RWKV)] * 6 + [full(a) for a in consts],
        out_specs=[tile(D), tile(D), tile(ROUTER_PAD)],
        compiler_params=pltpu.CompilerParams(
            dimension_semantics=("parallel", "parallel"), vmem_limit_bytes=VMEM_LIMIT),
        name="mix",
    )(x, ya, g, bv, of, ob, sg, *consts)


def _moe_kernel(be_ref, nu_ref, tok0_ref, tokn_ref, dst_ref, gt_ref, h_hbm, wg_ref, wu_ref, wd_ref, y_hbm,
                xbuf, ybuf, gsem, ssem):
    del be_ref
    i = pl.program_id(0)
    nb_used = nu_ref[0]
    slot = lax.rem(i, 2)

    def gather_copy(tok, r, s):
        return pltpu.make_async_copy(h_hbm.at[pl.ds(tok, 1)], xbuf.at[s, pl.ds(r, 1)], gsem.at[s])

    def scatter_copy(dst, r, s):
        return pltpu.make_async_copy(ybuf.at[s, pl.ds(r, 1)], y_hbm.at[pl.ds(dst, 1)], ssem.at[s])

    def start_gather(tok_ref, s):
        def body(r, c):
            gather_copy(tok_ref[0, 0, r], r, s).start()
            return c
        lax.fori_loop(0, MOE_ROWS, body, 0, unroll=8)

    def start_scatter(s):
        def body(r, c):
            scatter_copy(dst_ref[0, 0, r], r, s).start()
            return c
        lax.fori_loop(0, MOE_ROWS, body, 0, unroll=8)

    def wait_gather(s):
        pltpu.make_async_copy(h_hbm.at[pl.ds(0, MOE_ROWS)], xbuf.at[s], gsem.at[s]).wait()

    def wait_scatter(s):
        pltpu.make_async_copy(ybuf.at[s], y_hbm.at[pl.ds(0, MOE_ROWS)], ssem.at[s]).wait()

    @pl.when(i == 0)
    def _():
        plane = y_hbm.shape[0] // 2
        ybuf[...] = jnp.zeros_like(ybuf)
        for s in range(2):
            spare = pltpu.make_async_copy(
                ybuf.at[s], y_hbm.at[pl.ds((s + 1) * plane - MOE_ROWS, MOE_ROWS)], ssem.at[s])
            spare.start()
            spare.wait()

    @pl.when(jnp.logical_and(i == 0, nb_used > 0))
    def _():
        start_gather(tok0_ref, 0)

    @pl.when(i + 1 < nb_used)
    def _():
        start_gather(tokn_ref, 1 - slot)

    @pl.when(i < nb_used)
    def _():
        wait_gather(slot)

        @pl.when(i >= 2)
        def _():
            wait_scatter(slot)

        x = xbuf[slot].astype(BF16)
        a = _dot(x, wg_ref[0])
        u = _dot(x, wu_ref[0])
        hid = (a * _sigmoid(a)) * u
        ybuf[slot] = _dot(hid.astype(BF16), wd_ref[0]) * gt_ref[...]
        start_scatter(slot)

        @pl.when(i == nb_used - 1)
        def _():
            wait_scatter(slot)

            @pl.when(i >= 1)
            def _():
                wait_scatter(1 - slot)


def _moe_call(block_e, nb_used, src_tok, dst_row, gate_slot, h2, p):
    N, D = h2.shape
    P = src_tok.shape[0]
    n_blocks = P // MOE_ROWS
    smem_blk = lambda f: pl.BlockSpec((1, 1, MOE_ROWS), f, memory_space=pltpu.SMEM)
    grid_spec = pltpu.PrefetchScalarGridSpec(
        num_scalar_prefetch=2,
        grid=(n_blocks,),
        in_specs=[
            smem_blk(lambda i, be, nu: (0, 0, 0)),
            smem_blk(lambda i, be, nu: (jnp.minimum(i + 1, n_blocks - 1), 0, 0)),
            smem_blk(lambda i, be, nu: (i, 0, 0)),
            pl.BlockSpec((MOE_ROWS, 1), lambda i, be, nu: (i, 0)),
            pl.BlockSpec(memory_space=pl.ANY),
            pl.BlockSpec((1, D, D_EXPERT), lambda i, be, nu: (be[i], 0, 0)),
            pl.BlockSpec((1, D, D_EXPERT), lambda i, be, nu: (be[i], 0, 0)),
            pl.BlockSpec((1, D_EXPERT, D), lambda i, be, nu: (be[i], 0, 0)),
        ],
        out_specs=pl.BlockSpec(memory_space=pl.ANY),
        scratch_shapes=[pltpu.VMEM((2, MOE_ROWS, D), F32), pltpu.VMEM((2, MOE_ROWS, D), F32),
                        pltpu.SemaphoreType.DMA((2,)), pltpu.SemaphoreType.DMA((2,))],
    )
    tok3 = src_tok.reshape(n_blocks, 1, MOE_ROWS)
    return pl.pallas_call(
        _moe_kernel,
        out_shape=jax.ShapeDtypeStruct((2 * (N + MOE_ROWS), D), F32),
        grid_spec=grid_spec,
        compiler_params=pltpu.CompilerParams(
            dimension_semantics=("arbitrary",), vmem_limit_bytes=VMEM_LIMIT),
        name="moe",
    )(block_e, nb_used, tok3, tok3, dst_row.reshape(n_blocks, 1, MOE_ROWS), gate_slot, h2,
      p["moe_wg"], p["moe_wu"], p["moe_wd"])


def _route_and_moe(h2, logits, p):
    N, D = h2.shape
    glog = logits[:, :N_GROUPS]
    elog = logits[:, N_GROUPS:N_GROUPS + N_EXPERTS].reshape(N, N_GROUPS, EXPERTS_PER_GROUP)
    gprob = jax.nn.softmax(glog, axis=-1)
    grp = jnp.argmax(glog, axis=-1)
    p_grp = jnp.take_along_axis(gprob, grp[:, None], axis=-1)
    elog_g = jnp.take_along_axis(elog, grp[:, None, None], axis=1)[:, 0]
    top_v, top_i = lax.top_k(elog_g, 2)
    gate = (p_grp * jax.nn.softmax(top_v, axis=-1)).reshape(2 * N)
    eid = (grp[:, None] * EXPERTS_PER_GROUP + top_i).reshape(2 * N).astype(jnp.int32)

    M = 2 * N
    order = jnp.argsort(eid, stable=True).astype(jnp.int32)
    experts = jnp.arange(N_EXPERTS, dtype=jnp.int32)
    e_sorted = eid[order]
    starts = jnp.searchsorted(e_sorted, experts, side="left").astype(jnp.int32)
    counts = jnp.searchsorted(e_sorted, experts, side="right").astype(jnp.int32) - starts
    padded = (counts + MOE_ROWS - 1) // MOE_ROWS * MOE_ROWS
    pad_ends = jnp.cumsum(padded)
    pad_starts = pad_ends - padded
    n_blocks = (M + N_EXPERTS * (MOE_ROWS - 1) + MOE_ROWS - 1) // MOE_ROWS
    P = n_blocks * MOE_ROWS
    block_e = jnp.minimum(
        jnp.sum(pad_ends[None, :] <= (jnp.arange(n_blocks, dtype=jnp.int32) * MOE_ROWS)[:, None], axis=1),
        N_EXPERTS - 1).astype(jnp.int32)
    slot = jnp.arange(P, dtype=jnp.int32)
    e_slot = jnp.repeat(block_e, MOE_ROWS)
    off = slot - pad_starts[e_slot]
    valid = (off >= 0) & (off < counts[e_slot])
    src = jnp.clip(starts[e_slot] + off, 0, M - 1)
    assign = order[src]
    gate_slot = jnp.where(valid, gate[assign], 0.0)[:, None]
    plane = N + MOE_ROWS
    src_tok = jnp.where(valid, assign // 2, 0)
    spare = ((slot // MOE_ROWS) % 2) * plane + N + slot % MOE_ROWS
    dst_row = jnp.where(valid, (assign % 2) * plane + assign // 2, spare)
    nb_used = (pad_ends[-1] // MOE_ROWS).astype(jnp.int32).reshape(1)
    return _moe_call(block_e, nb_used, src_tok, dst_row, gate_slot, h2, p)


def _final_kernel(x_ref, m0_ref, m1_ref, w_ref, o_ref):
    x = x_ref[...] + m0_ref[...] + m1_ref[...]
    ms = jnp.mean(x * x, axis=-1, keepdims=True)
    o_ref[...] = x * lax.rsqrt(ms + NORM_EPS) * w_ref[...]


def _final_call(x1, moe2, w):
    N, D = x1.shape
    rows = min(PROJ_ROWS * 2, N)
    tile = pl.BlockSpec((rows, D), lambda i: (i, 0))
    moe = moe2.reshape(2, moe2.shape[0] // 2, D)
    return pl.pallas_call(
        _final_kernel,
        out_shape=jax.ShapeDtypeStruct((N, D), F32),
        grid=(N // rows,),
        in_specs=[tile, pl.BlockSpec((None, rows, D), lambda i: (0, i, 0)),
                  pl.BlockSpec((None, rows, D), lambda i: (1, i, 0)), pl.BlockSpec((1, D), lambda i: (0, 0))],
        out_specs=tile,
        compiler_params=pltpu.CompilerParams(
            dimension_semantics=("parallel",), vmem_limit_bytes=VMEM_LIMIT),
        name="final",
    )(x1, moe, moe, w)


def _block_diag_ones(width, head):
    idx = jnp.arange(width) // head
    return (idx[:, None] == idx[None, :]).astype(BF16)


def _prepare(norm1_w, w_in, mu_shift, rwkv_w0, rwkv_w2, rwkv_a0, rwkv_a2, rwkv_g2, rwkv_k_k, rwkv_k_a,
             rwkv_r_k, rwkv_lnx_w, rwkv_lnx_b, hgrn_lb, hgrn_gnorm_w, w_out, norm2_w, router_group_w,
             router_group_b, router_expert_w, router_expert_b, moe_w_gate, moe_w_up, moe_w_down, final_norm_w):
    row = lambda a: a.reshape(1, -1).astype(F32)
    lb_all = jnp.cumsum(jax.nn.softmax(hgrn_lb.astype(F32), axis=1), axis=1)
    rw = jnp.concatenate([router_group_w[0], router_expert_w[0]], axis=1)
    rw = jnp.pad(rw, ((0, 0), (0, ROUTER_PAD - rw.shape[1])))
    rw_hi = rw.astype(BF16)
    rw_lo = (rw - rw_hi.astype(F32)).astype(BF16)
    rb = jnp.pad(jnp.concatenate([router_group_b[0], router_expert_b[0]]), (0, ROUTER_PAD - N_GROUPS - N_EXPERTS))
    perm = jnp.arange(D_RWKV).reshape(RWKV_HEADS, RWKV_HEAD).T.reshape(-1)
    cols = jnp.concatenate([perm, C_K + perm, C_V + perm, jnp.arange(C_WD, C_RWKV_END)])
    col = lambda a: jnp.broadcast_to(a.astype(F32)[..., None], a.shape + (LANES,))
    head = jnp.arange(D_RWKV) % RWKV_HEADS
    return {
        "norm1_w": row(norm1_w[0]),
        "w_r": w_in[0][:, :C_RWKV_END].T[cols].astype(BF16),
        "w_h": w_in[0][:, C_RWKV_END:].astype(BF16),
        "mu": col(mu_shift[0][cols]),
        "w0": col(rwkv_w0[0][:, perm]), "w2": jnp.swapaxes(rwkv_w2[0], 1, 2)[:, perm].astype(BF16),
        "a0": col(rwkv_a0[0][:, perm]), "a2": jnp.swapaxes(rwkv_a2[0], 1, 2)[:, perm].astype(BF16),
        "g2": rwkv_g2[0][:, perm].astype(BF16),
        "k_k": col(rwkv_k_k[0][perm]), "k_a": col(rwkv_k_a[0][perm]),
        "r_k": col(rwkv_r_k[0].reshape(-1)[perm]),
        "lnx_w": row(rwkv_lnx_w[0][perm]), "lnx_b": row(rwkv_lnx_b[0][perm]),
        "lb": lb_all[:, 0],
        "gnorm_w": row(jnp.tile(hgrn_gnorm_w[0], HGRN_HEADS)),
        "hd64": (head[:, None] == head[None, :]).astype(BF16),
        "hd128": _block_diag_ones(D_HGRN, HGRN_HEAD),
        "w_out": jnp.concatenate([w_out[0][:D_RWKV][perm], w_out[0][D_RWKV:]]).astype(BF16),
        "norm2_w": row(norm2_w[0]),
        "rw_hi": rw_hi, "rw_lo": rw_lo, "rb": row(rb),
        "moe_wg": moe_w_gate[0].astype(BF16), "moe_wu": moe_w_up[0].astype(BF16),
        "moe_wd": moe_w_down[0].astype(BF16),
        "final_w": row(final_norm_w),
    }


def _forward(x, p):
    B, T, D = x.shape
    z, g, bv, q, lf, kf, iv, sg = _proj_call(x, p)

    chains = B * RWKV_HEADS
    vq = LANES // chains
    nv = RWKV_HEAD // (vq * SUBLANES)
    a, v = _to_chains_call(z, vq, nv * SUBLANES)
    ya = _from_chains_call(_rwkv_call(a, v, nv), B, vq, nv * SUBLANES)

    of, ob = _hgrn_call(q, iv, lf, kf)
    x1, h2, logits = _mix_call(x, ya, g, bv, of, ob, sg, p)
    N = B * T
    moe = _route_and_moe(h2.reshape(N, D), logits.reshape(N, ROUTER_PAD), p)
    return _final_call(x1.reshape(N, D), moe, p["final_w"]).reshape(B, T, D)


def kernel(x_prompt, x_sample, norm1_w, w_in, mu_shift, rwkv_w0, rwkv_w2, rwkv_a0, rwkv_a2, rwkv_g2, rwkv_k_k, rwkv_k_a, rwkv_r_k, rwkv_lnx_w, rwkv_lnx_b, hgrn_lb, hgrn_gnorm_w, w_out, norm2_w, router_group_w, router_group_b, router_expert_w, router_expert_b, moe_w_gate, moe_w_up, moe_w_down, final_norm_w):
    p = _prepare(norm1_w, w_in, mu_shift, rwkv_w0, rwkv_w2, rwkv_a0, rwkv_a2, rwkv_g2, rwkv_k_k, rwkv_k_a,
                 rwkv_r_k, rwkv_lnx_w, rwkv_lnx_b, hgrn_lb, hgrn_gnorm_w, w_out, norm2_w, router_group_w,
                 router_group_b, router_expert_w, router_expert_b, moe_w_gate, moe_w_up, moe_w_down,
                 final_norm_w)
    return (_forward(x_prompt, p), _forward(x_sample, p))
```

```python
import functools

import jax
import jax.numpy as jnp
from jax import lax
from jax.experimental import pallas as pl
from jax.experimental.pallas import tpu as pltpu

F32 = jnp.float32
BF16 = jnp.bfloat16

D_MODEL = 1024
D_RWKV = 512
RWKV_HEAD = 64
RWKV_HEADS = 8
D_HGRN = 512
HGRN_HEAD = 128
HGRN_HEADS = 4
DECAY_LORA = 64
AAA_LORA = 64
GATE_LORA = 128
N_GROUPS = 4
EXPERTS_PER_GROUP = 8
N_EXPERTS = 32
D_EXPERT = 512
NORM_EPS = 1e-6
HGRN_NORM_EPS = 1e-5
LNX_EPS = RWKV_HEAD * 1e-5

C_K = 512
C_V = 1024
C_WD = 1536
C_AD = 1664
C_GD = 1792
C_RWKV_END = 1920
D_IN = 4480
HG_Q = 0
HG_F = 512
HG_I = 1536
HG_G = 2048
D_HG_IN = 2560

SUBLANES = 8
LANES = 128
VMEM_LIMIT = 56 * 1024 * 1024

PROJ_ROWS = 256
HALO = 8
SCAN_STEPS = 64
HGRN_CHUNK = 64
HGRN_SUB = 8
LOG2E = 1.4426950408889634
HGRN_TILE = 512
MOE_ROWS = 256
ROUTER_PAD = 128
RELAYOUT_T = 128

Z_R, Z_KK, Z_W, Z_K, Z_B, Z_V = 0, 1, 2, 4, 6, 8
Z_COUNT = 9
SHIFT_R, SHIFT_W, SHIFT_K, SHIFT_B = 0, 1, 3, 2


def _dot(a, b):
    return jnp.dot(a, b, preferred_element_type=F32)


def _dot_nt(a, b):
    return lax.dot_general(a, b, (((1,), (1,)), ((), ())), preferred_element_type=F32)


def _dot_tn(a, b):
    return lax.dot_general(a, b, (((0,), (0,)), ((), ())), preferred_element_type=F32)


def _split2(a):
    hi = a.astype(BF16)
    lo = (a - hi.astype(F32)).astype(BF16)
    return hi, lo


def _split3(a):
    hi = a.astype(BF16)
    r1 = a - hi.astype(F32)
    mid = r1.astype(BF16)
    lo = (r1 - mid.astype(F32)).astype(BF16)
    return hi, mid, lo


def _dot_lhs2(a, b_bf16):
    hi, lo = _split2(a)
    return _dot(hi, b_bf16) + _dot(lo, b_bf16)


def _sigmoid(x):
    return 1.0 / (1.0 + jnp.exp(-x))


def _softplus(x):
    return jnp.maximum(x, 0.0) + jnp.log(1.0 + jnp.exp(-jnp.abs(x)))


def _lanes(c, n):
    if n == LANES:
        return c
    if n < LANES:
        return c[:, :n]
    return jnp.concatenate([c] * (n // LANES), axis=1)


def _dot_rhs2(a_bf16, b):
    hi, lo = _split2(b)
    return _dot(a_bf16, hi) + _dot(a_bf16, lo)


def _proj_kernel(x_ref, xp_ref, xn_ref, n1_ref, wr_ref, wh_ref, mu_ref, w0_ref, w2_ref, a0_ref, a2_ref,
                 g2_ref, kk_ref, ka_ref, rk_ref, lb_ref, hd_ref,
                 z_o, g_o, bv_o, q_o, lf_o, kf_o, iv_o, sg_o):
    i = pl.program_id(1)
    n_i = pl.num_programs(1)
    rows = x_ref.shape[1]

    def norm(xf):
        ms = jnp.mean(xf * xf, axis=-1, keepdims=True)
        return (xf * lax.rsqrt(ms + NORM_EPS) * n1_ref[...]).astype(BF16)

    h_c = norm(x_ref[0])
    h_h = norm(jnp.concatenate([xp_ref[0] * (i > 0).astype(F32), xn_ref[0] * (i < n_i - 1).astype(F32)], axis=0))

    wr = wr_ref[...]
    u_c = _dot_nt(wr, h_c)
    u_h = _dot_nt(wr, h_h)
    lane = lax.broadcasted_iota(jnp.int32, (1, rows), 1)
    u_prev = jnp.where(lane == 0, u_h[:, HALO - 1:HALO], pltpu.roll(u_c, 1, 1))
    u_next = jnp.where(lane == rows - 1, u_h[:, HALO:HALO + 1], pltpu.roll(u_c, rows - 1, 1))
    us = u_c + (0.5 * (u_prev + u_next) - u_c) * _lanes(mu_ref[...], rows)

    def put(j, a):
        z_o[j] = a.reshape(RWKV_HEAD, SUBLANES, rows)

    r = us[0:C_K]
    k = us[C_K:C_V]
    v = us[C_V:C_WD]
    hd = hd_ref[...]
    kk = k * _lanes(kk_ref[...], rows)
    kn = kk / jnp.maximum(jnp.sqrt(_dot_rhs2(hd, kk * kk)), 1e-12)
    gd = us[C_GD:C_RWKV_END]
    g_o[0] = _dot_tn(_sigmoid(gd).astype(BF16), g2_ref[...])
    put(Z_R, r)
    put(Z_KK, kn)
    put(Z_V, v)
    kd_sum = jnp.zeros_like(k)
    for d in range(2):
        wd = us[C_WD + d * DECAY_LORA:C_WD + (d + 1) * DECAY_LORA]
        ad = us[C_AD + d * AAA_LORA:C_AD + (d + 1) * AAA_LORA]
        zw = _lanes(w0_ref[d], rows) + _dot(w2_ref[d], jnp.tanh(wd).astype(BF16))
        w_log = -_softplus(-zw) - 0.5
        put(Z_W + d, jnp.exp(-jnp.exp(w_log)))
        a = _sigmoid(_lanes(a0_ref[d], rows) + _dot(a2_ref[d], ad.astype(BF16)))
        kd = k * (1.0 + (a - 1.0) * _lanes(ka_ref[...], rows))
        put(Z_K + d, kd)
        put(Z_B + d, kn * a)
        kd_sum = kd_sum + kd
    bonus = _dot_rhs2(hd, r * kd_sum * _lanes(rk_ref[...], rows))
    bv_o[0] = (bonus * v).T

    uh = _dot(h_c, wh_ref[...])
    uq = uh[:, HG_Q:HG_F]
    q_o[0] = uq * _sigmoid(uq)
    iv_o[0] = uh[:, HG_I:HG_G]
    ug = uh[:, HG_G:D_HG_IN]
    sg_o[0] = ug * _sigmoid(ug)
    for d in range(2):
        fr = uh[:, HG_F + d * D_HGRN:HG_F + (d + 1) * D_HGRN]
        lb = lb_ref[d:d + 1, :]
        f = lb + (1.0 - lb) * _sigmoid(fr)
        lf_o[d, 0] = jnp.log(f)
        kf_o[d, 0] = 1.0 - f


def _proj_call(x, p):
    B, T, D = x.shape
    rows = min(PROJ_ROWS, T)
    n_i = T // rows
    rb = rows // HALO

    def full(a):
        nd = a.ndim
        return pl.BlockSpec(a.shape, lambda b, i, _n=nd: (0,) * _n)

    tile = pl.BlockSpec((1, rows, D), lambda b, i: (b, i, 0))
    prev = pl.BlockSpec((1, HALO, D), lambda b, i: (b, jnp.maximum(i * rb - 1, 0), 0))
    nxt = pl.BlockSpec((1, HALO, D), lambda b, i: (b, jnp.minimum((i + 1) * rb, T // HALO - 1), 0))
    consts = [p["norm1_w"], p["w_r"], p["w_h"], p["mu"], p["w0"], p["w2"], p["a0"], p["a2"], p["g2"],
              p["k_k"], p["k_a"], p["r_k"], p["lb"], p["hd64"]]
    one = jax.ShapeDtypeStruct((B, T, D_RWKV), F32)
    two = jax.ShapeDtypeStruct((2, B, T, D_RWKV), F32)
    one_spec = pl.BlockSpec((1, rows, D_RWKV), lambda b, i: (b, i, 0))
    two_spec = pl.BlockSpec((2, 1, rows, D_RWKV), lambda b, i: (0, b, i, 0))
    z = jax.ShapeDtypeStruct((Z_COUNT, RWKV_HEAD, B * RWKV_HEADS, T), F32)
    z_spec = pl.BlockSpec((Z_COUNT, RWKV_HEAD, RWKV_HEADS, rows), lambda b, i: (0, 0, b, i))
    kinds = [z, one, one, one, two, two, one, one]
    return pl.pallas_call(
        _proj_kernel,
        out_shape=kinds,
        grid=(B, n_i),
        in_specs=[tile, prev, nxt] + [full(a) for a in consts],
        out_specs=[z_spec if s is z else (one_spec if s is one else two_spec) for s in kinds],
        compiler_params=pltpu.CompilerParams(
            dimension_semantics=("parallel", "arbitrary"), vmem_limit_bytes=VMEM_LIMIT),
        name="proj",
    )(x, x, x, *consts)


def _rwkv_kernel(r_ref, kk_ref, v_ref, w_ref, k_ref, b_ref, y_ref, s_ref, *, nv, steps):
    d = pl.program_id(0)
    i = pl.program_id(1)
    K = RWKV_HEAD
    n_par = max(1, SUBLANES // nv)

    @pl.when(i == 0)
    def _():
        s_ref[...] = jnp.zeros_like(s_ref)

    def row(ref, tt, k):
        return jnp.broadcast_to(ref[tt[0], k, pl.ds(tt[1], 1), :], (SUBLANES, LANES))

    def tree_sum(parts):
        parts = [x for x in parts if x is not None]
        while len(parts) > 1:
            parts = [parts[j] + parts[j + 1] if j + 1 < len(parts) else parts[j]
                     for j in range(0, len(parts), 2)]
        return parts[0]

    def step(j, carry):
        t = jnp.where(d == 0, j, steps - 1 - j)
        tt = (t // SUBLANES, t % SUBLANES)
        acc = [[None] * n_par for _ in range(nv)]
        for k in range(K):
            kk_row = row(kk_ref, tt, (k + SHIFT_W) % K)
            for vg in range(nv):
                term = s_ref[vg * K + k] * kk_row
                slot = k % n_par
                acc[vg][slot] = term if acc[vg][slot] is None else acc[vg][slot] + term
        sa = [-tree_sum(acc[vg]) for vg in range(nv)]
        val = [v_ref[t, pl.ds(vg * SUBLANES, SUBLANES), :] for vg in range(nv)]
        yacc = [[None] * n_par for _ in range(nv)]
        for k in range(K):
            w_row = row(w_ref, tt, (k + SHIFT_W) % K)
            b_row = row(b_ref, tt, (k + SHIFT_B) % K)
            k_row = row(k_ref, tt, (k + SHIFT_K) % K)
            r_row = row(r_ref, tt, (k + SHIFT_R) % K)
            for vg in range(nv):
                s_new = s_ref[vg * K + k] * w_row + sa[vg] * b_row + val[vg] * k_row
                s_ref[vg * K + k] = s_new
                term = s_new * r_row
                slot = k % n_par
                yacc[vg][slot] = term if yacc[vg][slot] is None else yacc[vg][slot] + term
        for vg in range(nv):
            y_ref[t, pl.ds(vg * SUBLANES, SUBLANES), :] = tree_sum(yacc[vg])
        return carry

    lax.fori_loop(0, steps, step, 0)


def _rwkv_call(a, v, nv):
    T = a.shape[1] * SUBLANES
    steps = min(SCAN_STEPS, T)
    n_t = T // steps
    nvp = nv * SUBLANES

    def tblk(d, i):
        return jnp.where(d == 0, i, n_t - 1 - i)

    kblock = (None, steps // SUBLANES, RWKV_HEAD, SUBLANES, LANES)

    def shared(j):
        return pl.BlockSpec(kblock, lambda d, i: (j, tblk(d, i), 0, 0, 0))

    def perdir(j):
        return pl.BlockSpec(kblock, lambda d, i: (j + d, tblk(d, i), 0, 0, 0))

    vspec = pl.BlockSpec((steps, nvp, LANES), lambda d, i: (tblk(d, i), 0, 0))
    yspec = pl.BlockSpec((None, steps, nvp, LANES), lambda d, i: (d, tblk(d, i), 0, 0))
    return pl.pallas_call(
        functools.partial(_rwkv_kernel, nv=nv, steps=steps),
        out_shape=jax.ShapeDtypeStruct((2, T, nvp, LANES), F32),
        grid=(2, n_t),
        in_specs=[shared(Z_R), shared(Z_KK), vspec, perdir(Z_W), perdir(Z_K), perdir(Z_B)],
        out_specs=yspec,
        scratch_shapes=[pltpu.VMEM((nv * RWKV_HEAD, SUBLANES, LANES), F32)],
        compiler_params=pltpu.CompilerParams(
            dimension_semantics=("parallel", "arbitrary"), vmem_limit_bytes=VMEM_LIMIT),
        name="rwkv",
    )(a, a, v, a, a, a)


def _plane_shift(j):
    return jnp.where(j < Z_KK, SHIFT_R, jnp.where(j < Z_K, SHIFT_W, jnp.where(j < Z_B, SHIFT_K, SHIFT_B)))


def _to_chains_k_kernel(z_ref, o_ref, *, vq):
    tl = z_ref.shape[2]
    shift = _plane_shift(pl.program_id(0))
    for p in range(RWKV_HEAD):
        a = z_ref[p]
        a = (jnp.concatenate([a] * vq, axis=0) if vq > 1 else a).T
        o_ref[:, (p + shift) % RWKV_HEAD] = a.reshape(tl // SUBLANES, SUBLANES, LANES)


def _to_chains_v_kernel(z_ref, o_ref, *, vq, nvp):
    for p in range(nvp):
        o_ref[:, p, :] = jnp.concatenate([z_ref[q * nvp + p] for q in range(vq)], axis=0).T


def _to_chains_call(z, vq, nvp):
    n_arr, planes, C, T = z.shape
    tl = min(RELAYOUT_T, T)
    a = pl.pallas_call(
        functools.partial(_to_chains_k_kernel, vq=vq),
        out_shape=jax.ShapeDtypeStruct((Z_V, T // SUBLANES, planes, SUBLANES, LANES), F32),
        grid=(Z_V, T // tl),
        in_specs=[pl.BlockSpec((None, planes, C, tl), lambda j, t: (j, 0, 0, t))],
        out_specs=pl.BlockSpec((None, tl // SUBLANES, planes, SUBLANES, LANES), lambda j, t: (j, t, 0, 0, 0)),
        compiler_params=pltpu.CompilerParams(
            dimension_semantics=("parallel", "parallel"), vmem_limit_bytes=VMEM_LIMIT),
        name="to_chains_k",
    )(z)
    v = pl.pallas_call(
        functools.partial(_to_chains_v_kernel, vq=vq, nvp=nvp),
        out_shape=jax.ShapeDtypeStruct((T, nvp, LANES), F32),
        grid=(T // tl,),
        in_specs=[pl.BlockSpec((None, planes, C, tl), lambda t: (Z_V, 0, 0, t))],
        out_specs=pl.BlockSpec((tl, nvp, LANES), lambda t: (t, 0, 0)),
        compiler_params=pltpu.CompilerParams(
            dimension_semantics=("parallel",), vmem_limit_bytes=VMEM_LIMIT),
        name="to_chains_v",
    )(z)
    return a, v


def _from_chains_kernel(y_ref, o_ref, zs, *, vq, nvp):
    B = o_ref.shape[0]
    C = B * RWKV_HEADS
    tl = o_ref.shape[1]
    for p in range(nvp):
        zs[p] = (y_ref[0, :, p, :] + y_ref[1, :, p, :]).T
    for b in range(B):
        rows = [zs[:, pl.ds(q * C + b * RWKV_HEADS, RWKV_HEADS), :] for q in range(vq)]
        slab = jnp.concatenate(rows, axis=0) if vq > 1 else rows[0]
        o_ref[b] = slab.reshape(D_RWKV, tl).T


def _from_chains_call(y, B, vq, nvp):
    T = y.shape[1]
    tl = min(RELAYOUT_T, T)
    return pl.pallas_call(
        functools.partial(_from_chains_kernel, vq=vq, nvp=nvp),
        out_shape=jax.ShapeDtypeStruct((B, T, D_RWKV), F32),
        grid=(T // tl,),
        in_specs=[pl.BlockSpec((2, tl, nvp, LANES), lambda t: (0, t, 0, 0))],
        out_specs=pl.BlockSpec((B, tl, D_RWKV), lambda t: (0, t, 0)),
        scratch_shapes=[pltpu.VMEM((nvp, LANES, tl), F32)],
        compiler_params=pltpu.CompilerParams(
            dimension_semantics=("parallel",), vmem_limit_bytes=VMEM_LIMIT),
        name="from_chains",
    )(y)


def _hgrn_direction(q, kf, val, lf, st_ref, tri, off_mask, sel, ones, rev, sub):
    C = q.shape[0]
    l_hi, l_mid, l_lo = _split3(lf)
    G = (_dot(tri, l_hi) + _dot(tri, l_mid) + _dot(tri, l_lo)) * LOG2E
    last = 0 if rev else C - 1
    g_tot = G[last:last + 1, :]
    outs = []
    for h in range(HGRN_HEADS):
        sl = slice(h * HGRN_HEAD, (h + 1) * HGRN_HEAD)
        Gh, qh, kh, vh = G[:, sl], q[:, sl], kf[:, sl], val[:, sl]
        st = st_ref[h]
        o = _dot_nt((qh * jnp.exp2(Gh)).astype(BF16), st.astype(BF16))
        a_off = None
        for lv, L in enumerate(_hgrn_levels(C, sub)):
            mid = L // 2 if rev else L // 2 - 1
            refs = [jnp.broadcast_to(Gh[m * L + mid:m * L + mid + 1, :], (L, HGRN_HEAD)) for m in range(C // L)]
            ref = jnp.concatenate(refs, axis=0) if len(refs) > 1 else refs[0]
            ql = qh * jnp.exp2(jnp.minimum(Gh - ref, 0.0))
            kl = kh * jnp.exp2(jnp.minimum(ref - Gh, 0.0))
            al = _dot_nt(ql.astype(BF16), kl.astype(BF16)) * off_mask[lv]
            a_off = al if a_off is None else a_off + al
        if a_off is not None:
            o = o + _dot(a_off.astype(BF16), vh.astype(BF16))
        parts = []
        for t in range(C):
            b0 = (t // sub) * sub
            rel = jnp.exp2(jnp.minimum(Gh[t:t + 1, :] - Gh[b0:b0 + sub, :], 0.0))
            parts.append(rel * kh[b0:b0 + sub, :] * qh[t:t + 1, :])
        ab = _dot(jnp.concatenate(parts, axis=0).astype(BF16), ones)
        wv = jnp.concatenate(
            [ab[t * sub:(t + 1) * sub] * vh[(t // sub) * sub:(t // sub + 1) * sub] for t in range(C)], axis=0)
        o = o + _dot(sel, wv.astype(BF16))
        outs.append(o)
        kdec = kh * jnp.exp2(g_tot[:, sl] - Gh)
        st_ref[h] = st * jnp.exp2(g_tot[:, sl]) + _dot_tn(vh.astype(BF16), kdec.astype(BF16))
    return jnp.concatenate(outs, axis=-1)


def _hgrn_kernel(qf_ref, vf_ref, lff_ref, kff_ref, qb_ref, vb_ref, lfb_ref, kfb_ref,
                 trif_ref, trib_ref, offf_ref, offb_ref, self_ref, selb_ref, ones_ref,
                 of_ref, ob_ref, stf_ref, stb_ref, *, chunk, sub):
    i = pl.program_id(1)
    n_c = qf_ref.shape[1] // chunk

    @pl.when(i == 0)
    def _():
        stf_ref[...] = jnp.zeros_like(stf_ref)
        stb_ref[...] = jnp.zeros_like(stb_ref)

    def body(c, carry):
        cf = pl.multiple_of(c * chunk, chunk)
        cb = pl.multiple_of((n_c - 1 - c) * chunk, chunk)
        of_ref[0, pl.ds(cf, chunk), :] = _hgrn_direction(
            qf_ref[0, pl.ds(cf, chunk), :], kff_ref[0, pl.ds(cf, chunk), :], vf_ref[0, pl.ds(cf, chunk), :],
            lff_ref[0, pl.ds(cf, chunk), :], stf_ref, trif_ref[...], offf_ref[...], self_ref[...], ones_ref[...],
            rev=False, sub=sub)
        ob_ref[0, pl.ds(cb, chunk), :] = _hgrn_direction(
            qb_ref[0, pl.ds(cb, chunk), :], kfb_ref[0, pl.ds(cb, chunk), :], vb_ref[0, pl.ds(cb, chunk), :],
            lfb_ref[0, pl.ds(cb, chunk), :], stb_ref, trib_ref[...], offb_ref[...], selb_ref[...], ones_ref[...],
            rev=True, sub=sub)
        return carry

    lax.fori_loop(0, n_c, body, 0)


def _hgrn_levels(chunk, sub):
    levels = []
    L = 2 * sub
    while L <= chunk:
        levels.append(L)
        L *= 2
    return levels


def _hgrn_consts(chunk, sub):
    t = jnp.arange(chunk)
    tri_f = (t[None, :] <= t[:, None])
    off_f = jnp.stack([(t[:, None] // L == t[None, :] // L) & (t[:, None] % L >= L // 2) & (t[None, :] % L < L // 2)
                       for L in _hgrn_levels(chunk, sub)] or [jnp.zeros((chunk, chunk), bool)])
    col = jnp.arange(chunk * sub)
    same_t = (col // sub)[None, :] == t[:, None]
    j = (col % sub)[None, :]
    sel_f = same_t & (j <= (t % sub)[:, None])
    sel_b = same_t & (j >= (t % sub)[:, None])
    return [tri_f.astype(BF16), tri_f.T.astype(BF16), off_f.astype(F32), jnp.swapaxes(off_f, 1, 2).astype(F32),
            sel_f.astype(BF16), sel_b.astype(BF16), jnp.ones((HGRN_HEAD, HGRN_HEAD), BF16)]


def _hgrn_call(q, iv, lf, kf):
    B, T, _ = q.shape
    tile = min(HGRN_TILE, T)
    chunk = min(HGRN_CHUNK, tile)
    sub = min(HGRN_SUB, chunk)
    n_i = T // tile
    consts = _hgrn_consts(chunk, sub)
    fwd = pl.BlockSpec((1, tile, D_HGRN), lambda b, i: (b, i, 0))
    bwd = pl.BlockSpec((1, tile, D_HGRN), lambda b, i: (b, n_i - 1 - i, 0))
    fwd2 = pl.BlockSpec((None, 1, tile, D_HGRN), lambda b, i: (0, b, i, 0))
    bwd2 = pl.BlockSpec((None, 1, tile, D_HGRN), lambda b, i: (1, b, n_i - 1 - i, 0))
    full = [pl.BlockSpec(a.shape, lambda b, i, _n=a.ndim: (0,) * _n) for a in consts]
    out = jax.ShapeDtypeStruct((B, T, D_HGRN), F32)
    state = pltpu.VMEM((HGRN_HEADS, HGRN_HEAD, HGRN_HEAD), F32)
    return pl.pallas_call(
        functools.partial(_hgrn_kernel, chunk=chunk, sub=sub),
        out_shape=[out, out],
        grid=(B, n_i),
        in_specs=[fwd, fwd, fwd2, fwd2, bwd, bwd, bwd2, bwd2] + full,
        out_specs=[fwd, bwd],
        scratch_shapes=[state, state],
        compiler_params=pltpu.CompilerParams(
            dimension_semantics=("parallel", "arbitrary"), vmem_limit_bytes=VMEM_LIMIT),
        name="hgrn",
    )(q, iv, lf, kf, q, iv, lf, kf, *consts)


def _mix_kernel(x_ref, ya_ref, g_ref, bv_ref, of_ref, ob_ref, sg_ref, lnw_ref, lnb_ref, gnw_ref,
                hd64_ref, hd128_ref, wo_ref, n2_ref, rwh_ref, rwl_ref, rb_ref,
                x1_o, h2_o, lg_o):
    hd64 = hd64_ref[...]
    y = ya_ref[0]
    inv = 1.0 / RWKV_HEAD
    mean = _dot_lhs2(y, hd64) * inv
    yc = y - mean
    var = _dot_lhs2(yc * yc, hd64) * inv
    ya = (yc * lax.rsqrt(var + LNX_EPS)) * lnw_ref[...] + lnb_ref[...]
    ya = (ya + bv_ref[0]) * g_ref[0]
    o = of_ref[0] + ob_ref[0]
    ms = _dot_lhs2(o * o, hd128_ref[...]) * (1.0 / HGRN_HEAD)
    yb = o * lax.rsqrt(ms + HGRN_NORM_EPS) * gnw_ref[...] * sg_ref[0]
    mixed = jnp.concatenate([ya, yb], axis=-1).astype(BF16)
    x1 = x_ref[0] + _dot(mixed, wo_ref[...])
    x1_o[0] = x1
    ms2 = jnp.mean(x1 * x1, axis=-1, keepdims=True)
    h2 = x1 * lax.rsqrt(ms2 + NORM_EPS) * n2_ref[...]
    h2_o[0] = h2
    h_hi, h_lo = _split2(h2)
    w_hi = rwh_ref[...]
    lg_o[0] = _dot(h_hi, w_hi) + _dot(h_lo, w_hi) + _dot(h_hi, rwl_ref[...]) + rb_ref[...]


def _mix_call(x, ya, g, bv, of, ob, sg, p):
    B, T, D = x.shape
    rows = min(PROJ_ROWS, T)

    def full(a):
        nd = a.ndim
        return pl.BlockSpec(a.shape, lambda b, i, _n=nd: (0,) * _n)

    def tile(w):
        return pl.BlockSpec((1, rows, w), lambda b, i: (b, i, 0))

    consts = [p["lnx_w"], p["lnx_b"], p["gnorm_w"], p["hd64"], p["hd128"], p["w_out"], p["norm2_w"],
              p["rw_hi"], p["rw_lo"], p["rb"]]
    return pl.pallas_call(
        _mix_kernel,
        out_shape=[jax.ShapeDtypeStruct((B, T, D), F32), jax.ShapeDtypeStruct((B, T, D), F32),
                   jax.ShapeDtypeStruct((B, T, ROUTER_PAD), F32)],
        grid=(B, T // rows),
        in_specs=[tile(D)] + [tile(D_RWKV)] * 6 + [full(a) for a in consts],
        out_specs=[tile(D), tile(D), tile(ROUTER_PAD)],
        compiler_params=pltpu.CompilerParams(
            dimension_semantics=("parallel", "parallel"), vmem_limit_bytes=VMEM_LIMIT),
        name="mix",
    )(x, ya, g, bv, of, ob, sg, *consts)


def _moe_kernel(be_ref, nu_ref, tok0_ref, tokn_ref, dst_ref, gt_ref, h_hbm, wg_ref, wu_ref, wd_ref, y_hbm,
                xbuf, ybuf, gsem, ssem):
    del be_ref
    i = pl.program_id(0)
    nb_used = nu_ref[0]
    slot = lax.rem(i, 2)

    def gather_copy(tok, r, s):
        return pltpu.make_async_copy(h_hbm.at[pl.ds(tok, 1)], xbuf.at[s, pl.ds(r, 1)], gsem.at[s])

    def scatter_copy(dst, r, s):
        return pltpu.make_async_copy(ybuf.at[s, pl.ds(r, 1)], y_hbm.at[pl.ds(dst, 1)], ssem.at[s])

    def start_gather(tok_ref, s):
        def body(r, c):
            gather_copy(tok_ref[0, 0, r], r, s).start()
            return c
        lax.fori_loop(0, MOE_ROWS, body, 0, unroll=8)

    def start_scatter(s):
        def body(r, c):
            scatter_copy(dst_ref[0, 0, r], r, s).start()
            return c
        lax.fori_loop(0, MOE_ROWS, body, 0, unroll=8)

    def wait_gather(s):
        pltpu.make_async_copy(h_hbm.at[pl.ds(0, MOE_ROWS)], xbuf.at[s], gsem.at[s]).wait()

    def wait_scatter(s):
        pltpu.make_async_copy(ybuf.at[s], y_hbm.at[pl.ds(0, MOE_ROWS)], ssem.at[s]).wait()

    @pl.when(i == 0)
    def _():
        plane = y_hbm.shape[0] // 2
        ybuf[...] = jnp.zeros_like(ybuf)
        for s in range(2):
            spare = pltpu.make_async_copy(
                ybuf.at[s], y_hbm.at[pl.ds((s + 1) * plane - MOE_ROWS, MOE_ROWS)], ssem.at[s])
            spare.start()
            spare.wait()

    @pl.when(jnp.logical_and(i == 0, nb_used > 0))
    def _():
        start_gather(tok0_ref, 0)

    @pl.when(i + 1 < nb_used)
    def _():
        start_gather(tokn_ref, 1 - slot)

    @pl.when(i < nb_used)
    def _():
        wait_gather(slot)

        @pl.when(i >= 2)
        def _():
            wait_scatter(slot)

        x = xbuf[slot].astype(BF16)
        a = _dot(x, wg_ref[0])
        u = _dot(x, wu_ref[0])
        hid = (a * _sigmoid(a)) * u
        ybuf[slot] = _dot(hid.astype(BF16), wd_ref[0]) * gt_ref[...]
        start_scatter(slot)

        @pl.when(i == nb_used - 1)
        def _():
            wait_scatter(slot)

            @pl.when(i >= 1)
            def _():
                wait_scatter(1 - slot)


def _moe_call(block_e, nb_used, src_tok, dst_row, gate_slot, h2, p):
    N, D = h2.shape
    P = src_tok.shape[0]
    n_blocks = P // MOE_ROWS
    smem_blk = lambda f: pl.BlockSpec((1, 1, MOE_ROWS), f, memory_space=pltpu.SMEM)
    grid_spec = pltpu.PrefetchScalarGridSpec(
        num_scalar_prefetch=2,
        grid=(n_blocks,),
        in_specs=[
            smem_blk(lambda i, be, nu: (0, 0, 0)),
            smem_blk(lambda i, be, nu: (jnp.minimum(i + 1, n_blocks - 1), 0, 0)),
            smem_blk(lambda i, be, nu: (i, 0, 0)),
            pl.BlockSpec((MOE_ROWS, 1), lambda i, be, nu: (i, 0)),
            pl.BlockSpec(memory_space=pl.ANY),
            pl.BlockSpec((1, D, D_EXPERT), lambda i, be, nu: (be[i], 0, 0)),
            pl.BlockSpec((1, D, D_EXPERT), lambda i, be, nu: (be[i], 0, 0)),
            pl.BlockSpec((1, D_EXPERT, D), lambda i, be, nu: (be[i], 0, 0)),
        ],
        out_specs=pl.BlockSpec(memory_space=pl.ANY),
        scratch_shapes=[pltpu.VMEM((2, MOE_ROWS, D), F32), pltpu.VMEM((2, MOE_ROWS, D), F32),
                        pltpu.SemaphoreType.DMA((2,)), pltpu.SemaphoreType.DMA((2,))],
    )
    tok3 = src_tok.reshape(n_blocks, 1, MOE_ROWS)
    return pl.pallas_call(
        _moe_kernel,
        out_shape=jax.ShapeDtypeStruct((2 * (N + MOE_ROWS), D), F32),
        grid_spec=grid_spec,
        compiler_params=pltpu.CompilerParams(
            dimension_semantics=("arbitrary",), vmem_limit_bytes=VMEM_LIMIT),
        name="moe",
    )(block_e, nb_used, tok3, tok3, dst_row.reshape(n_blocks, 1, MOE_ROWS), gate_slot, h2,
      p["moe_wg"], p["moe_wu"], p["moe_wd"])


def _route_and_moe(h2, logits, p):
    N, D = h2.shape
    glog = logits[:, :N_GROUPS]
    elog = logits[:, N_GROUPS:N_GROUPS + N_EXPERTS].reshape(N, N_GROUPS, EXPERTS_PER_GROUP)
    gprob = jax.nn.softmax(glog, axis=-1)
    grp = jnp.argmax(glog, axis=-1)
    p_grp = jnp.take_along_axis(gprob, grp[:, None], axis=-1)
    elog_g = jnp.take_along_axis(elog, grp[:, None, None], axis=1)[:, 0]
    top_v, top_i = lax.top_k(elog_g, 2)
    gate = (p_grp * jax.nn.softmax(top_v, axis=-1)).reshape(2 * N)
    eid = (grp[:, None] * EXPERTS_PER_GROUP + top_i).reshape(2 * N).astype(jnp.int32)

    M = 2 * N
    order = jnp.argsort(eid, stable=True).astype(jnp.int32)
    experts = jnp.arange(N_EXPERTS, dtype=jnp.int32)
    e_sorted = eid[order]
    starts = jnp.searchsorted(e_sorted, experts, side="left").astype(jnp.int32)
    counts = jnp.searchsorted(e_sorted, experts, side="right").astype(jnp.int32) - starts
    padded = (counts + MOE_ROWS - 1) // MOE_ROWS * MOE_ROWS
    pad_ends = jnp.cumsum(padded)
    pad_starts = pad_ends - padded
    n_blocks = (M + N_EXPERTS * (MOE_ROWS - 1) + MOE_ROWS - 1) // MOE_ROWS
    P = n_blocks * MOE_ROWS
    block_e = jnp.minimum(
        jnp.sum(pad_ends[None, :] <= (jnp.arange(n_blocks, dtype=jnp.int32) * MOE_ROWS)[:, None], axis=1),
        N_EXPERTS - 1).astype(jnp.int32)
    slot = jnp.arange(P, dtype=jnp.int32)
    e_slot = jnp.repeat(block_e, MOE_ROWS)
    off = slot - pad_starts[e_slot]
    valid = (off >= 0) & (off < counts[e_slot])
    src = jnp.clip(starts[e_slot] + off, 0, M - 1)
    assign = order[src]
    gate_slot = jnp.where(valid, gate[assign], 0.0)[:, None]
    plane = N + MOE_ROWS
    src_tok = jnp.where(valid, assign // 2, 0)
    spare = ((slot // MOE_ROWS) % 2) * plane + N + slot % MOE_ROWS
    dst_row = jnp.where(valid, (assign % 2) * plane + assign // 2, spare)
    nb_used = (pad_ends[-1] // MOE_ROWS).astype(jnp.int32).reshape(1)
    return _moe_call(block_e, nb_used, src_tok, dst_row, gate_slot, h2, p)


def _final_kernel(x_ref, m0_ref, m1_ref, w_ref, o_ref):
    x = x_ref[...] + m0_ref[...] + m1_ref[...]
    ms = jnp.mean(x * x, axis=-1, keepdims=True)
    o_ref[...] = x * lax.rsqrt(ms + NORM_EPS) * w_ref[...]


def _final_call(x1, moe2, w):
    N, D = x1.shape
    rows = min(PROJ_ROWS * 2, N)
    tile = pl.BlockSpec((rows, D), lambda i: (i, 0))
    moe = moe2.reshape(2, moe2.shape[0] // 2, D)
    return pl.pallas_call(
        _final_kernel,
        out_shape=jax.ShapeDtypeStruct((N, D), F32),
        grid=(N // rows,),
        in_specs=[tile, pl.BlockSpec((None, rows, D), lambda i: (0, i, 0)),
                  pl.BlockSpec((None, rows, D), lambda i: (1, i, 0)), pl.BlockSpec((1, D), lambda i: (0, 0))],
        out_specs=tile,
        compiler_params=pltpu.CompilerParams(
            dimension_semantics=("parallel",), vmem_limit_bytes=VMEM_LIMIT),
        name="final",
    )(x1, moe, moe, w)


def _block_diag_ones(width, head):
    idx = jnp.arange(width) // head
    return (idx[:, None] == idx[None, :]).astype(BF16)


def _prepare(norm1_w, w_in, mu_shift, rwkv_w0, rwkv_w2, rwkv_a0, rwkv_a2, rwkv_g2, rwkv_k_k, rwkv_k_a,
             rwkv_r_k, rwkv_lnx_w, rwkv_lnx_b, hgrn_lb, hgrn_gnorm_w, w_out, norm2_w, router_group_w,
             router_group_b, router_expert_w, router_expert_b, moe_w_gate, moe_w_up, moe_w_down, final_norm_w):
    row = lambda a: a.reshape(1, -1).astype(F32)
    lb_all = jnp.cumsum(jax.nn.softmax(hgrn_lb.astype(F32), axis=1), axis=1)
    rw = jnp.concatenate([router_group_w[0], router_expert_w[0]], axis=1)
    rw = jnp.pad(rw, ((0, 0), (0, ROUTER_PAD - rw.shape[1])))
    rw_hi = rw.astype(BF16)
    rw_lo = (rw - rw_hi.astype(F32)).astype(BF16)
    rb = jnp.pad(jnp.concatenate([router_group_b[0], router_expert_b[0]]), (0, ROUTER_PAD - N_GROUPS - N_EXPERTS))
    perm = jnp.arange(D_RWKV).reshape(RWKV_HEADS, RWKV_HEAD).T.reshape(-1)
    cols = jnp.concatenate([perm, C_K + perm, C_V + perm, jnp.arange(C_WD, C_RWKV_END)])
    col = lambda a: jnp.broadcast_to(a.astype(F32)[..., None], a.shape + (LANES,))
    head = jnp.arange(D_RWKV) % RWKV_HEADS
    return {
        "norm1_w": row(norm1_w[0]),
        "w_r": w_in[0][:, :C_RWKV_END].T[cols].astype(BF16),
        "w_h": w_in[0][:, C_RWKV_END:].astype(BF16),
        "mu": col(mu_shift[0][cols]),
        "w0": col(rwkv_w0[0][:, perm]), "w2": jnp.swapaxes(rwkv_w2[0], 1, 2)[:, perm].astype(BF16),
        "a0": col(rwkv_a0[0][:, perm]), "a2": jnp.swapaxes(rwkv_a2[0], 1, 2)[:, perm].astype(BF16),
        "g2": rwkv_g2[0][:, perm].astype(BF16),
        "k_k": col(rwkv_k_k[0][perm]), "k_a": col(rwkv_k_a[0][perm]),
        "r_k": col(rwkv_r_k[0].reshape(-1)[perm]),
        "lnx_w": row(rwkv_lnx_w[0][perm]), "lnx_b": row(rwkv_lnx_b[0][perm]),
        "lb": lb_all[:, 0],
        "gnorm_w": row(jnp.tile(hgrn_gnorm_w[0], HGRN_HEADS)),
        "hd64": (head[:, None] == head[None, :]).astype(BF16),
        "hd128": _block_diag_ones(D_HGRN, HGRN_HEAD),
        "w_out": jnp.concatenate([w_out[0][:D_RWKV][perm], w_out[0][D_RWKV:]]).astype(BF16),
        "norm2_w": row(norm2_w[0]),
        "rw_hi": rw_hi, "rw_lo": rw_lo, "rb": row(rb),
        "moe_wg": moe_w_gate[0].astype(BF16), "moe_wu": moe_w_up[0].astype(BF16),
        "moe_wd": moe_w_down[0].astype(BF16),
        "final_w": row(final_norm_w),
    }


def _forward(x, p):
    B, T, D = x.shape
    z, g, bv, q, lf, kf, iv, sg = _proj_call(x, p)

    chains = B * RWKV_HEADS
    vq = LANES // chains
    nv = RWKV_HEAD // (vq * SUBLANES)
    a, v = _to_chains_call(z, vq, nv * SUBLANES)
    ya = _from_chains_call(_rwkv_call(a, v, nv), B, vq, nv * SUBLANES)

    of, ob = _hgrn_call(q, iv, lf, kf)
    x1, h2, logits = _mix_call(x, ya, g, bv, of, ob, sg, p)
    N = B * T
    moe = _route_and_moe(h2.reshape(N, D), logits.reshape(N, ROUTER_PAD), p)
    return _final_call(x1.reshape(N, D), moe, p["final_w"]).reshape(B, T, D)


def kernel(x_prompt, x_sample, norm1_w, w_in, mu_shift, rwkv_w0, rwkv_w2, rwkv_a0, rwkv_a2, rwkv_g2, rwkv_k_k, rwkv_k_a, rwkv_r_k, rwkv_lnx_w, rwkv_lnx_b, hgrn_lb, hgrn_gnorm_w, w_out, norm2_w, router_group_w, router_group_b, router_expert_w, router_expert_b, moe_w_gate, moe_w_up, moe_w_down, final_norm_w):
    p = _prepare(norm1_w, w_in, mu_shift, rwkv_w0, rwkv_w2, rwkv_a0, rwkv_a2, rwkv_g2, rwkv_k_k, rwkv_k_a,
                 rwkv_r_k, rwkv_lnx_w, rwkv_lnx_b, hgrn_lb, hgrn_gnorm_w, w_out, norm2_w, router_group_w,
                 router_group_b, router_expert_w, router_expert_b, moe_w_gate, moe_w_up, moe_w_down,
                 final_norm_w)
    return (_forward(x_prompt, p), _forward(x_sample, p))
```

```python
import functools

import jax
import jax.numpy as jnp
from jax import lax
from jax.experimental import pallas as pl
from jax.experimental.pallas import tpu as pltpu

F32 = jnp.float32
BF16 = jnp.bfloat16

D_MODEL = 1024
D_RWKV = 512
RWKV_HEAD = 64
RWKV_HEADS = 8
D_HGRN = 512
HGRN_HEAD = 128
HGRN_HEADS = 4
DECAY_LORA = 64
AAA_LORA = 64
GATE_LORA = 128
N_GROUPS = 4
EXPERTS_PER_GROUP = 8
N_EXPERTS = 32
D_EXPERT = 512
NORM_EPS = 1e-6
HGRN_NORM_EPS = 1e-5
LNX_EPS = RWKV_HEAD * 1e-5

C_K = 512
C_V = 1024
C_WD = 1536
C_AD = 1664
C_GD = 1792
C_RWKV_END = 1920
D_IN = 4480
HG_Q = 0
HG_F = 512
HG_I = 1536
HG_G = 2048
D_HG_IN = 2560

SUBLANES = 8
LANES = 128
VMEM_LIMIT = 56 * 1024 * 1024

PROJ_ROWS = 256
HALO = 8
SCAN_STEPS = 64
HGRN_CHUNK = 64
HGRN_SUB = 8
LOG2E = 1.4426950408889634
HGRN_TILE = 512
MOE_ROWS = 256
ROUTER_PAD = 128
RELAYOUT_T = 128

Z_R, Z_KK, Z_W, Z_K, Z_B, Z_V = 0, 1, 2, 4, 6, 8
Z_COUNT = 9
SHIFT_R, SHIFT_W, SHIFT_K, SHIFT_B = 0, 1, 3, 2


def _dot(a, b):
    return jnp.dot(a, b, preferred_element_type=F32)


def _dot_nt(a, b):
    return lax.dot_general(a, b, (((1,), (1,)), ((), ())), preferred_element_type=F32)


def _dot_tn(a, b):
    return lax.dot_general(a, b, (((0,), (0,)), ((), ())), preferred_element_type=F32)


def _split2(a):
    hi = a.astype(BF16)
    lo = (a - hi.astype(F32)).astype(BF16)
    return hi, lo


def _split3(a):
    hi = a.astype(BF16)
    r1 = a - hi.astype(F32)
    mid = r1.astype(BF16)
    lo = (r1 - mid.astype(F32)).astype(BF16)
    return hi, mid, lo


def _dot_lhs2(a, b_bf16):
    hi, lo = _split2(a)
    return _dot(hi, b_bf16) + _dot(lo, b_bf16)


def _sigmoid(x):
    return 1.0 / (1.0 + jnp.exp(-x))


def _softplus(x):
    return jnp.maximum(x, 0.0) + jnp.log(1.0 + jnp.exp(-jnp.abs(x)))


def _lanes(c, n):
    if n == LANES:
        return c
    if n < LANES:
        return c[:, :n]
    return jnp.concatenate([c] * (n // LANES), axis=1)


def _dot_rhs2(a_bf16, b):
    hi, lo = _split2(b)
    return _dot(a_bf16, hi) + _dot(a_bf16, lo)


def _proj_kernel(x_ref, xp_ref, xn_ref, n1_ref, wr_ref, wh_ref, mu_ref, w0_ref, w2_ref, a0_ref, a2_ref,
                 g2_ref, kk_ref, ka_ref, rk_ref, lb_ref, hd_ref,
                 z_o, g_o, bv_o, q_o, lf_o, kf_o, iv_o, sg_o):
    i = pl.program_id(1)
    n_i = pl.num_programs(1)
    rows = x_ref.shape[1]

    def norm(xf):
        ms = jnp.mean(xf * xf, axis=-1, keepdims=True)
        return (xf * lax.rsqrt(ms + NORM_EPS) * n1_ref[...]).astype(BF16)

    h_c = norm(x_ref[0])
    h_h = norm(jnp.concatenate([xp_ref[0] * (i > 0).astype(F32), xn_ref[0] * (i < n_i - 1).astype(F32)], axis=0))

    wr = wr_ref[...]
    u_c = _dot_nt(wr, h_c)
    u_h = _dot_nt(wr, h_h)
    lane = lax.broadcasted_iota(jnp.int32, (1, rows), 1)
    u_prev = jnp.where(lane == 0, u_h[:, HALO - 1:HALO], pltpu.roll(u_c, 1, 1))
    u_next = jnp.where(lane == rows - 1, u_h[:, HALO:HALO + 1], pltpu.roll(u_c, rows - 1, 1))
    us = u_c + (0.5 * (u_prev + u_next) - u_c) * _lanes(mu_ref[...], rows)

    def put(j, a):
        z_o[j] = a.reshape(RWKV_HEAD, SUBLANES, rows)

    r = us[0:C_K]
    k = us[C_K:C_V]
    v = us[C_V:C_WD]
    hd = hd_ref[...]
    kk = k * _lanes(kk_ref[...], rows)
    kn = kk / jnp.maximum(jnp.sqrt(_dot_rhs2(hd, kk * kk)), 1e-12)
    gd = us[C_GD:C_RWKV_END]
    g_o[0] = _dot_tn(_sigmoid(gd).astype(BF16), g2_ref[...])
    put(Z_R, r)
    put(Z_KK, kn)
    put(Z_V, v)
    kd_sum = jnp.zeros_like(k)
    for d in range(2):
        wd = us[C_WD + d * DECAY_LORA:C_WD + (d + 1) * DECAY_LORA]
        ad = us[C_AD + d * AAA_LORA:C_AD + (d + 1) * AAA_LORA]
        zw = _lanes(w0_ref[d], rows) + _dot(w2_ref[d], jnp.tanh(wd).astype(BF16))
        w_log = -_softplus(-zw) - 0.5
        put(Z_W + d, jnp.exp(-jnp.exp(w_log)))
        a = _sigmoid(_lanes(a0_ref[d], rows) + _dot(a2_ref[d], ad.astype(BF16)))
        kd = k * (1.0 + (a - 1.0) * _lanes(ka_ref[...], rows))
        put(Z_K + d, kd)
        put(Z_B + d, kn * a)
        kd_sum = kd_sum + kd
    bonus = _dot_rhs2(hd, r * kd_sum * _lanes(rk_ref[...], rows))
    bv_o[0] = (bonus * v).T

    uh = _dot(h_c, wh_ref[...])
    uq = uh[:, HG_Q:HG_F]
    q_o[0] = uq * _sigmoid(uq)
    iv_o[0] = uh[:, HG_I:HG_G]
    ug = uh[:, HG_G:D_HG_IN]
    sg_o[0] = ug * _sigmoid(ug)
    for d in range(2):
        fr = uh[:, HG_F + d * D_HGRN:HG_F + (d + 1) * D_HGRN]
        lb = lb_ref[d:d + 1, :]
        f = lb + (1.0 - lb) * _sigmoid(fr)
        lf_o[d, 0] = jnp.log(f)
        kf_o[d, 0] = 1.0 - f


def _proj_call(x, p):
    B, T, D = x.shape
    rows = min(PROJ_ROWS, T)
    n_i = T // rows
    rb = rows // HALO

    def full(a):
        nd = a.ndim
        return pl.BlockSpec(a.shape, lambda b, i, _n=nd: (0,) * _n)

    tile = pl.BlockSpec((1, rows, D), lambda b, i: (b, i, 0))
    prev = pl.BlockSpec((1, HALO, D), lambda b, i: (b, jnp.maximum(i * rb - 1, 0), 0))
    nxt = pl.BlockSpec((1, HALO, D), lambda b, i: (b, jnp.minimum((i + 1) * rb, T // HALO - 1), 0))
    consts = [p["norm1_w"], p["w_r"], p["w_h"], p["mu"], p["w0"], p["w2"], p["a0"], p["a2"], p["g2"],
              p["k_k"], p["k_a"], p["r_k"], p["lb"], p["hd64"]]
    one = jax.ShapeDtypeStruct((B, T, D_RWKV), F32)
    two = jax.ShapeDtypeStruct((2, B, T, D_RWKV), F32)
    one_spec = pl.BlockSpec((1, rows, D_RWKV), lambda b, i: (b, i, 0))
    two_spec = pl.BlockSpec((2, 1, rows, D_RWKV), lambda b, i: (0, b, i, 0))
    z = jax.ShapeDtypeStruct((Z_COUNT, RWKV_HEAD, B * RWKV_HEADS, T), F32)
    z_spec = pl.BlockSpec((Z_COUNT, RWKV_HEAD, RWKV_HEADS, rows), lambda b, i: (0, 0, b, i))
    kinds = [z, one, one, one, two, two, one, one]
    return pl.pallas_call(
        _proj_kernel,
        out_shape=kinds,
        grid=(B, n_i),
        in_specs=[tile, prev, nxt] + [full(a) for a in consts],
        out_specs=[z_spec if s is z else (one_spec if s is one else two_spec) for s in kinds],
        compiler_params=pltpu.CompilerParams(
            dimension_semantics=("parallel", "arbitrary"), vmem_limit_bytes=VMEM_LIMIT),
        name="proj",
    )(x, x, x, *consts)


def _rwkv_kernel(r_ref, kk_ref, v_ref, w_ref, k_ref, b_ref, y_ref, s_ref, *, nv, steps):
    d = pl.program_id(0)
    i = pl.program_id(1)
    K = RWKV_HEAD
    n_par = max(1, SUBLANES // nv)

    @pl.when(i == 0)
    def _():
        s_ref[...] = jnp.zeros_like(s_ref)

    def row(ref, tt, k):
        return jnp.broadcast_to(ref[tt[0], k, pl.ds(tt[1], 1), :], (SUBLANES, LANES))

    def tree_sum(parts):
        parts = [x for x in parts if x is not None]
        while len(parts) > 1:
            parts = [parts[j] + parts[j + 1] if j + 1 < len(parts) else parts[j]
                     for j in range(0, len(parts), 2)]
        return parts[0]

    def step(j, carry):
        t = jnp.where(d == 0, j, steps - 1 - j)
        tt = (t // SUBLANES, t % SUBLANES)
        acc = [[None] * n_par for _ in range(nv)]
        for k in range(K):
            kk_row = row(kk_ref, tt, (k + SHIFT_W) % K)
            for vg in range(nv):
                term = s_ref[vg * K + k] * kk_row
                slot = k % n_par
                acc[vg][slot] = term if acc[vg][slot] is None else acc[vg][slot] + term
        sa = [-tree_sum(acc[vg]) for vg in range(nv)]
        val = [v_ref[t, pl.ds(vg * SUBLANES, SUBLANES), :] for vg in range(nv)]
        yacc = [[None] * n_par for _ in range(nv)]
        for k in range(K):
            w_row = row(w_ref, tt, (k + SHIFT_W) % K)
            b_row = row(b_ref, tt, (k + SHIFT_B) % K)
            k_row = row(k_ref, tt, (k + SHIFT_K) % K)
            r_row = row(r_ref, tt, (k + SHIFT_R) % K)
            for vg in range(nv):
                s_new = s_ref[vg * K + k] * w_row + sa[vg] * b_row + val[vg] * k_row
                s_ref[vg * K + k] = s_new
                term = s_new * r_row
                slot = k % n_par
                yacc[vg][slot] = term if yacc[vg][slot] is None else yacc[vg][slot] + term
        for vg in range(nv):
            y_ref[t, pl.ds(vg * SUBLANES, SUBLANES), :] = tree_sum(yacc[vg])
        return carry

    lax.fori_loop(0, steps, step, 0)


def _rwkv_call(a, v, nv):
    T = a.shape[1] * SUBLANES
    steps = min(SCAN_STEPS, T)
    n_t = T // steps
    nvp = nv * SUBLANES

    def tblk(d, i):
        return jnp.where(d == 0, i, n_t - 1 - i)

    kblock = (None, steps // SUBLANES, RWKV_HEAD, SUBLANES, LANES)

    def shared(j):
        return pl.BlockSpec(kblock, lambda d, i: (j, tblk(d, i), 0, 0, 0))

    def perdir(j):
        return pl.BlockSpec(kblock, lambda d, i: (j + d, tblk(d, i), 0, 0, 0))

    vspec = pl.BlockSpec((steps, nvp, LANES), lambda d, i: (tblk(d, i), 0, 0))
    yspec = pl.BlockSpec((None, steps, nvp, LANES), lambda d, i: (d, tblk(d, i), 0, 0))
    return pl.pallas_call(
        functools.partial(_rwkv_kernel, nv=nv, steps=steps),
        out_shape=jax.ShapeDtypeStruct((2, T, nvp, LANES), F32),
        grid=(2, n_t),
        in_specs=[shared(Z_R), shared(Z_KK), vspec, perdir(Z_W), perdir(Z_K), perdir(Z_B)],
        out_specs=yspec,
        scratch_shapes=[pltpu.VMEM((nv * RWKV_HEAD, SUBLANES, LANES), F32)],
        compiler_params=pltpu.CompilerParams(
            dimension_semantics=("parallel", "arbitrary"), vmem_limit_bytes=VMEM_LIMIT),
        name="rwkv",
    )(a, a, v, a, a, a)


def _plane_shift(j):
    return jnp.where(j < Z_KK, SHIFT_R, jnp.where(j < Z_K, SHIFT_W, jnp.where(j < Z_B, SHIFT_K, SHIFT_B)))


def _to_chains_k_kernel(z_ref, o_ref, *, vq):
    tl = z_ref.shape[2]
    shift = _plane_shift(pl.program_id(0))
    for p in range(RWKV_HEAD):
        a = z_ref[p]
        a = (jnp.concatenate([a] * vq, axis=0) if vq > 1 else a).T
        o_ref[:, (p + shift) % RWKV_HEAD] = a.reshape(tl // SUBLANES, SUBLANES, LANES)


def _to_chains_v_kernel(z_ref, o_ref, *, vq, nvp):
    for p in range(nvp):
        o_ref[:, p, :] = jnp.concatenate([z_ref[q * nvp + p] for q in range(vq)], axis=0).T


def _to_chains_call(z, vq, nvp):
    n_arr, planes, C, T = z.shape
    tl = min(RELAYOUT_T, T)
    a = pl.pallas_call(
        functools.partial(_to_chains_k_kernel, vq=vq),
        out_shape=jax.ShapeDtypeStruct((Z_V, T // SUBLANES, planes, SUBLANES, LANES), F32),
        grid=(Z_V, T // tl),
        in_specs=[pl.BlockSpec((None, planes, C, tl), lambda j, t: (j, 0, 0, t))],
        out_specs=pl.BlockSpec((None, tl // SUBLANES, planes, SUBLANES, LANES), lambda j, t: (j, t, 0, 0, 0)),
        compiler_params=pltpu.CompilerParams(
            dimension_semantics=("parallel", "parallel"), vmem_limit_bytes=VMEM_LIMIT),
        name="to_chains_k",
    )(z)
    v = pl.pallas_call(
        functools.partial(_to_chains_v_kernel, vq=vq, nvp=nvp),
        out_shape=jax.ShapeDtypeStruct((T, nvp, LANES), F32),
        grid=(T // tl,),
        in_specs=[pl.BlockSpec((None, planes, C, tl), lambda t: (Z_V, 0, 0, t))],
        out_specs=pl.BlockSpec((tl, nvp, LANES), lambda t: (t, 0, 0)),
        compiler_params=pltpu.CompilerParams(
            dimension_semantics=("parallel",), vmem_limit_bytes=VMEM_LIMIT),
        name="to_chains_v",
    )(z)
    return a, v


def _from_chains_kernel(y_ref, o_ref, zs, *, vq, nvp):
    B = o_ref.shape[0]
    C = B * RWKV_HEADS
    tl = o_ref.shape[1]
    for p in range(nvp):
        zs[p] = (y_ref[0, :, p, :] + y_ref[1, :, p, :]).T
    for b in range(B):
        rows = [zs[:, pl.ds(q * C + b * RWKV_HEADS, RWKV_HEADS), :] for q in range(vq)]
        slab = jnp.concatenate(rows, axis=0) if vq > 1 else rows[0]
        o_ref[b] = slab.reshape(D_RWKV, tl).T


def _from_chains_call(y, B, vq, nvp):
    T = y.shape[1]
    tl = min(RELAYOUT_T, T)
    return pl.pallas_call(
        functools.partial(_from_chains_kernel, vq=vq, nvp=nvp),
        out_shape=jax.ShapeDtypeStruct((B, T, D_RWKV), F32),
        grid=(T // tl,),
        in_specs=[pl.BlockSpec((2, tl, nvp, LANES), lambda t: (0, t, 0, 0))],
        out_specs=pl.BlockSpec((B, tl, D_RWKV), lambda t: (0, t, 0)),
        scratch_shapes=[pltpu.VMEM((nvp, LANES, tl), F32)],
        compiler_params=pltpu.CompilerParams(
            dimension_semantics=("parallel",), vmem_limit_bytes=VMEM_LIMIT),
        name="from_chains",
    )(y)


def _hgrn_direction(q, kf, val, lf, st_ref, tri, off_mask, sel, ones, rev, sub):
    C = q.shape[0]
    l_hi, l_mid, l_lo = _split3(lf)
    G = (_dot(tri, l_hi) + _dot(tri, l_mid) + _dot(tri, l_lo)) * LOG2E
    last = 0 if rev else C - 1
    g_tot = G[last:last + 1, :]
    outs = []
    for h in range(HGRN_HEADS):
        sl = slice(h * HGRN_HEAD, (h + 1) * HGRN_HEAD)
        Gh, qh, kh, vh = G[:, sl], q[:, sl], kf[:, sl], val[:, sl]
        st = st_ref[h]
        o = _dot_nt((qh * jnp.exp2(Gh)).astype(BF16), st.astype(BF16))
        a_off = None
        for lv, L in enumerate(_hgrn_levels(C, sub)):
            mid = L // 2 if rev else L // 2 - 1
            refs = [jnp.broadcast_to(Gh[m * L + mid:m * L + mid + 1, :], (L, HGRN_HEAD)) for m in range(C // L)]
            ref = jnp.concatenate(refs, axis=0) if len(refs) > 1 else refs[0]
            ql = qh * jnp.exp2(jnp.minimum(Gh - ref, 0.0))
            kl = kh * jnp.exp2(jnp.minimum(ref - Gh, 0.0))
            al = _dot_nt(ql.astype(BF16), kl.astype(BF16)) * off_mask[lv]
            a_off = al if a_off is None else a_off + al
        if a_off is not None:
            o = o + _dot(a_off.astype(BF16), vh.astype(BF16))
        parts = []
        for t in range(C):
            b0 = (t // sub) * sub
            rel = jnp.exp2(jnp.minimum(Gh[t:t + 1, :] - Gh[b0:b0 + sub, :], 0.0))
            parts.append(rel * kh[b0:b0 + sub, :] * qh[t:t + 1, :])
        ab = _dot(jnp.concatenate(parts, axis=0).astype(BF16), ones)
        wv = jnp.concatenate(
            [ab[t * sub:(t + 1) * sub] * vh[(t // sub) * sub:(t // sub + 1) * sub] for t in range(C)], axis=0)
        o = o + _dot(sel, wv.astype(BF16))
        outs.append(o)
        kdec = kh * jnp.exp2(g_tot[:, sl] - Gh)
        st_ref[h] = st * jnp.exp2(g_tot[:, sl]) + _dot_tn(vh.astype(BF16), kdec.astype(BF16))
    return jnp.concatenate(outs, axis=-1)


def _hgrn_kernel(qf_ref, vf_ref, lff_ref, kff_ref, qb_ref, vb_ref, lfb_ref, kfb_ref,
                 trif_ref, trib_ref, offf_ref, offb_ref, self_ref, selb_ref, ones_ref,
                 of_ref, ob_ref, stf_ref, stb_ref, *, chunk, sub):
    i = pl.program_id(1)
    n_c = qf_ref.shape[1] // chunk

    @pl.when(i == 0)
    def _():
        stf_ref[...] = jnp.zeros_like(stf_ref)
        stb_ref[...] = jnp.zeros_like(stb_ref)

    def body(c, carry):
        cf = pl.multiple_of(c * chunk, chunk)
        cb = pl.multiple_of((n_c - 1 - c) * chunk, chunk)
        of_ref[0, pl.ds(cf, chunk), :] = _hgrn_direction(
            qf_ref[0, pl.ds(cf, chunk), :], kff_ref[0, pl.ds(cf, chunk), :], vf_ref[0, pl.ds(cf, chunk), :],
            lff_ref[0, pl.ds(cf, chunk), :], stf_ref, trif_ref[...], offf_ref[...], self_ref[...], ones_ref[...],
            rev=False, sub=sub)
        ob_ref[0, pl.ds(cb, chunk), :] = _hgrn_direction(
            qb_ref[0, pl.ds(cb, chunk), :], kfb_ref[0, pl.ds(cb, chunk), :], vb_ref[0, pl.ds(cb, chunk), :],
            lfb_ref[0, pl.ds(cb, chunk), :], stb_ref, trib_ref[...], offb_ref[...], selb_ref[...], ones_ref[...],
            rev=True, sub=sub)
        return carry

    lax.fori_loop(0, n_c, body, 0)


def _hgrn_levels(chunk, sub):
    levels = []
    L = 2 * sub
    while L <= chunk:
        levels.append(L)
        L *= 2
    return levels


def _hgrn_consts(chunk, sub):
    t = jnp.arange(chunk)
    tri_f = (t[None, :] <= t[:, None])
    off_f = jnp.stack([(t[:, None] // L == t[None, :] // L) & (t[:, None] % L >= L // 2) & (t[None, :] % L < L // 2)
                       for L in _hgrn_levels(chunk, sub)] or [jnp.zeros((chunk, chunk), bool)])
    col = jnp.arange(chunk * sub)
    same_t = (col // sub)[None, :] == t[:, None]
    j = (col % sub)[None, :]
    sel_f = same_t & (j <= (t % sub)[:, None])
    sel_b = same_t & (j >= (t % sub)[:, None])
    return [tri_f.astype(BF16), tri_f.T.astype(BF16), off_f.astype(F32), jnp.swapaxes(off_f, 1, 2).astype(F32),
            sel_f.astype(BF16), sel_b.astype(BF16), jnp.ones((HGRN_HEAD, HGRN_HEAD), BF16)]


def _hgrn_call(q, iv, lf, kf):
    B, T, _ = q.shape
    tile = min(HGRN_TILE, T)
    chunk = min(HGRN_CHUNK, tile)
    sub = min(HGRN_SUB, chunk)
    n_i = T // tile
    consts = _hgrn_consts(chunk, sub)
    fwd = pl.BlockSpec((1, tile, D_HGRN), lambda b, i: (b, i, 0))
    bwd = pl.BlockSpec((1, tile, D_HGRN), lambda b, i: (b, n_i - 1 - i, 0))
    fwd2 = pl.BlockSpec((None, 1, tile, D_HGRN), lambda b, i: (0, b, i, 0))
    bwd2 = pl.BlockSpec((None, 1, tile, D_HGRN), lambda b, i: (1, b, n_i - 1 - i, 0))
    full = [pl.BlockSpec(a.shape, lambda b, i, _n=a.ndim: (0,) * _n) for a in consts]
    out = jax.ShapeDtypeStruct((B, T, D_HGRN), F32)
    state = pltpu.VMEM((HGRN_HEADS, HGRN_HEAD, HGRN_HEAD), F32)
    return pl.pallas_call(
        functools.partial(_hgrn_kernel, chunk=chunk, sub=sub),
        out_shape=[out, out],
        grid=(B, n_i),
        in_specs=[fwd, fwd, fwd2, fwd2, bwd, bwd, bwd2, bwd2] + full,
        out_specs=[fwd, bwd],
        scratch_shapes=[state, state],
        compiler_params=pltpu.CompilerParams(
            dimension_semantics=("parallel", "arbitrary"), vmem_limit_bytes=VMEM_LIMIT),
        name="hgrn",
    )(q, iv, lf, kf, q, iv, lf, kf, *consts)


def _mix_kernel(x_ref, ya_ref, g_ref, bv_ref, of_ref, ob_ref, sg_ref, lnw_ref, lnb_ref, gnw_ref,
                hd64_ref, hd128_ref, wo_ref, n2_ref, rwh_ref, rwl_ref, rb_ref,
                x1_o, h2_o, lg_o):
    hd64 = hd64_ref[...]
    y = ya_ref[0]
    inv = 1.0 / RWKV_HEAD
    mean = _dot_lhs2(y, hd64) * inv
    yc = y - mean
    var = _dot_lhs2(yc * yc, hd64) * inv
    ya = (yc * lax.rsqrt(var + LNX_EPS)) * lnw_ref[...] + lnb_ref[...]
    ya = (ya + bv_ref[0]) * g_ref[0]
    o = of_ref[0] + ob_ref[0]
    ms = _dot_lhs2(o * o, hd128_ref[...]) * (1.0 / HGRN_HEAD)
    yb = o * lax.rsqrt(ms + HGRN_NORM_EPS) * gnw_ref[...] * sg_ref[0]
    mixed = jnp.concatenate([ya, yb], axis=-1).astype(BF16)
    x1 = x_ref[0] + _dot(mixed, wo_ref[...])
    x1_o[0] = x1
    ms2 = jnp.mean(x1 * x1, axis=-1, keepdims=True)
    h2 = x1 * lax.rsqrt(ms2 + NORM_EPS) * n2_ref[...]
    h2_o[0] = h2
    h_hi, h_lo = _split2(h2)
    w_hi = rwh_ref[...]
    lg_o[0] = (_dot_nt(w_hi, h_hi) + _dot_nt(w_hi, h_lo) + _dot_nt(rwl_ref[...], h_hi)
               + _lanes(rb_ref[...], h2.shape[0]))


def _mix_call(x, ya, g, bv, of, ob, sg, p):
    B, T, D = x.shape
    rows = min(PROJ_ROWS, T)

    def full(a):
        nd = a.ndim
        return pl.BlockSpec(a.shape, lambda b, i, _n=nd: (0,) * _n)

    def tile(w):
        return pl.BlockSpec((1, rows, w), lambda b, i: (b, i, 0))

    consts = [p["lnx_w"], p["lnx_b"], p["gnorm_w"], p["hd64"], p["hd128"], p["w_out"], p["norm2_w"],
              p["rw_hi"], p["rw_lo"], p["rb"]]
    return pl.pallas_call(
        _mix_kernel,
        out_shape=[jax.ShapeDtypeStruct((B, T, D), F32), jax.ShapeDtypeStruct((B, T, D), F32),
                   jax.ShapeDtypeStruct((B, ROUTER_PAD, T), F32)],
        grid=(B, T // rows),
        in_specs=[tile(D)] + [tile(D_RWKV)] * 6 + [full(a) for a in consts],
        out_specs=[tile(D), tile(D), pl.BlockSpec((1, ROUTER_PAD, rows), lambda b, i: (b, 0, i))],
        compiler_params=pltpu.CompilerParams(
            dimension_semantics=("parallel", "parallel"), vmem_limit_bytes=VMEM_LIMIT),
        name="mix",
    )(x, ya, g, bv, of, ob, sg, *consts)


def _moe_kernel(be_ref, nu_ref, tok0_ref, tokn_ref, dst_ref, gt_ref, h_hbm, wg_ref, wu_ref, wd_ref, y_hbm,
                xbuf, ybuf, gsem, ssem):
    del be_ref
    i = pl.program_id(0)
    nb_used = nu_ref[0]
    slot = lax.rem(i, 2)

    def gather_copy(tok, r, s):
        return pltpu.make_async_copy(h_hbm.at[pl.ds(tok, 1)], xbuf.at[s, pl.ds(r, 1)], gsem.at[s])

    def scatter_copy(dst, r, s):
        return pltpu.make_async_copy(ybuf.at[s, pl.ds(r, 1)], y_hbm.at[pl.ds(dst, 1)], ssem.at[s])

    def start_gather(tok_ref, s):
        def body(r, c):
            gather_copy(tok_ref[0, 0, r], r, s).start()
            return c
        lax.fori_loop(0, MOE_ROWS, body, 0, unroll=8)

    def start_scatter(s):
        def body(r, c):
            scatter_copy(dst_ref[0, 0, r], r, s).start()
            return c
        lax.fori_loop(0, MOE_ROWS, body, 0, unroll=8)

    def wait_gather(s):
        pltpu.make_async_copy(h_hbm.at[pl.ds(0, MOE_ROWS)], xbuf.at[s], gsem.at[s]).wait()

    def wait_scatter(s):
        pltpu.make_async_copy(ybuf.at[s], y_hbm.at[pl.ds(0, MOE_ROWS)], ssem.at[s]).wait()

    @pl.when(i == 0)
    def _():
        plane = y_hbm.shape[0] // 2
        ybuf[...] = jnp.zeros_like(ybuf)
        for s in range(2):
            spare = pltpu.make_async_copy(
                ybuf.at[s], y_hbm.at[pl.ds((s + 1) * plane - MOE_ROWS, MOE_ROWS)], ssem.at[s])
            spare.start()
            spare.wait()

    @pl.when(jnp.logical_and(i == 0, nb_used > 0))
    def _():
        start_gather(tok0_ref, 0)

    @pl.when(i + 1 < nb_used)
    def _():
        start_gather(tokn_ref, 1 - slot)

    @pl.when(i < nb_used)
    def _():
        wait_gather(slot)

        @pl.when(i >= 2)
        def _():
            wait_scatter(slot)

        x = xbuf[slot].astype(BF16)
        a = _dot(x, wg_ref[0])
        u = _dot(x, wu_ref[0])
        hid = (a * _sigmoid(a)) * u
        ybuf[slot] = _dot(hid.astype(BF16), wd_ref[0]) * gt_ref[...]
        start_scatter(slot)

        @pl.when(i == nb_used - 1)
        def _():
            wait_scatter(slot)

            @pl.when(i >= 1)
            def _():
                wait_scatter(1 - slot)


def _moe_call(block_e, nb_used, src_tok, dst_row, gate_slot, h2, p):
    N, D = h2.shape
    P = src_tok.shape[0]
    n_blocks = P // MOE_ROWS
    smem_blk = lambda f: pl.BlockSpec((1, 1, MOE_ROWS), f, memory_space=pltpu.SMEM)
    grid_spec = pltpu.PrefetchScalarGridSpec(
        num_scalar_prefetch=2,
        grid=(n_blocks,),
        in_specs=[
            smem_blk(lambda i, be, nu: (0, 0, 0)),
            smem_blk(lambda i, be, nu: (jnp.minimum(i + 1, n_blocks - 1), 0, 0)),
            smem_blk(lambda i, be, nu: (i, 0, 0)),
            pl.BlockSpec((MOE_ROWS, 1), lambda i, be, nu: (i, 0)),
            pl.BlockSpec(memory_space=pl.ANY),
            pl.BlockSpec((1, D, D_EXPERT), lambda i, be, nu: (be[i], 0, 0)),
            pl.BlockSpec((1, D, D_EXPERT), lambda i, be, nu: (be[i], 0, 0)),
            pl.BlockSpec((1, D_EXPERT, D), lambda i, be, nu: (be[i], 0, 0)),
        ],
        out_specs=pl.BlockSpec(memory_space=pl.ANY),
        scratch_shapes=[pltpu.VMEM((2, MOE_ROWS, D), F32), pltpu.VMEM((2, MOE_ROWS, D), F32),
                        pltpu.SemaphoreType.DMA((2,)), pltpu.SemaphoreType.DMA((2,))],
    )
    tok3 = src_tok.reshape(n_blocks, 1, MOE_ROWS)
    return pl.pallas_call(
        _moe_kernel,
        out_shape=jax.ShapeDtypeStruct((2 * (N + MOE_ROWS), D), F32),
        grid_spec=grid_spec,
        compiler_params=pltpu.CompilerParams(
            dimension_semantics=("arbitrary",), vmem_limit_bytes=VMEM_LIMIT),
        name="moe",
    )(block_e, nb_used, tok3, tok3, dst_row.reshape(n_blocks, 1, MOE_ROWS), gate_slot, h2,
      p["moe_wg"], p["moe_wu"], p["moe_wd"])


def _route_and_moe(h2, logits, p):
    N, D = h2.shape
    glog = logits[:N_GROUPS]
    elog = logits[N_GROUPS:N_GROUPS + N_EXPERTS].reshape(N_GROUPS, EXPERTS_PER_GROUP, N)
    grp = jnp.argmax(glog, axis=0)
    p_grp = 1.0 / jnp.sum(jnp.exp(glog - jnp.max(glog, axis=0, keepdims=True)), axis=0)
    gsel = jnp.arange(N_GROUPS)[:, None, None] == grp[None, None, :]
    elog_g = jnp.sum(jnp.where(gsel, elog, 0.0), axis=0)
    i1 = jnp.argmax(elog_g, axis=0)
    v1 = jnp.max(elog_g, axis=0)
    rest = jnp.where(jnp.arange(EXPERTS_PER_GROUP)[:, None] == i1[None, :], -jnp.inf, elog_g)
    i2 = jnp.argmax(rest, axis=0)
    e21 = jnp.exp(jnp.max(rest, axis=0) - v1)
    gate = jnp.stack([p_grp / (1.0 + e21), p_grp * e21 / (1.0 + e21)], axis=1).reshape(2 * N)
    eid = (grp[:, None] * EXPERTS_PER_GROUP + jnp.stack([i1, i2], axis=1)).reshape(2 * N).astype(jnp.int32)

    M = 2 * N
    order = jnp.argsort(eid, stable=True).astype(jnp.int32)
    experts = jnp.arange(N_EXPERTS, dtype=jnp.int32)
    e_sorted = eid[order]
    starts = jnp.searchsorted(e_sorted, experts, side="left").astype(jnp.int32)
    counts = jnp.searchsorted(e_sorted, experts, side="right").astype(jnp.int32) - starts
    padded = (counts + MOE_ROWS - 1) // MOE_ROWS * MOE_ROWS
    pad_ends = jnp.cumsum(padded)
    pad_starts = pad_ends - padded
    n_blocks = (M + N_EXPERTS * (MOE_ROWS - 1) + MOE_ROWS - 1) // MOE_ROWS
    P = n_blocks * MOE_ROWS
    block_e = jnp.minimum(
        jnp.sum(pad_ends[None, :] <= (jnp.arange(n_blocks, dtype=jnp.int32) * MOE_ROWS)[:, None], axis=1),
        N_EXPERTS - 1).astype(jnp.int32)
    slot = jnp.arange(P, dtype=jnp.int32)
    e_slot = jnp.repeat(block_e, MOE_ROWS)
    off = slot - pad_starts[e_slot]
    valid = (off >= 0) & (off < counts[e_slot])
    src = jnp.clip(starts[e_slot] + off, 0, M - 1)
    assign = order[src]
    gate_slot = jnp.where(valid, gate[assign], 0.0)[:, None]
    plane = N + MOE_ROWS
    src_tok = jnp.where(valid, assign // 2, 0)
    spare = ((slot // MOE_ROWS) % 2) * plane + N + slot % MOE_ROWS
    dst_row = jnp.where(valid, (assign % 2) * plane + assign // 2, spare)
    nb_used = (pad_ends[-1] // MOE_ROWS).astype(jnp.int32).reshape(1)
    return _moe_call(block_e, nb_used, src_tok, dst_row, gate_slot, h2, p)


def _final_kernel(x_ref, m0_ref, m1_ref, w_ref, o_ref):
    x = x_ref[...] + m0_ref[...] + m1_ref[...]
    ms = jnp.mean(x * x, axis=-1, keepdims=True)
    o_ref[...] = x * lax.rsqrt(ms + NORM_EPS) * w_ref[...]


def _final_call(x1, moe2, w):
    N, D = x1.shape
    rows = min(PROJ_ROWS * 2, N)
    tile = pl.BlockSpec((rows, D), lambda i: (i, 0))
    moe = moe2.reshape(2, moe2.shape[0] // 2, D)
    return pl.pallas_call(
        _final_kernel,
        out_shape=jax.ShapeDtypeStruct((N, D), F32),
        grid=(N // rows,),
        in_specs=[tile, pl.BlockSpec((None, rows, D), lambda i: (0, i, 0)),
                  pl.BlockSpec((None, rows, D), lambda i: (1, i, 0)), pl.BlockSpec((1, D), lambda i: (0, 0))],
        out_specs=tile,
        compiler_params=pltpu.CompilerParams(
            dimension_semantics=("parallel",), vmem_limit_bytes=VMEM_LIMIT),
        name="final",
    )(x1, moe, moe, w)


def _block_diag_ones(width, head):
    idx = jnp.arange(width) // head
    return (idx[:, None] == idx[None, :]).astype(BF16)


def _prepare(norm1_w, w_in, mu_shift, rwkv_w0, rwkv_w2, rwkv_a0, rwkv_a2, rwkv_g2, rwkv_k_k, rwkv_k_a,
             rwkv_r_k, rwkv_lnx_w, rwkv_lnx_b, hgrn_lb, hgrn_gnorm_w, w_out, norm2_w, router_group_w,
             router_group_b, router_expert_w, router_expert_b, moe_w_gate, moe_w_up, moe_w_down, final_norm_w):
    row = lambda a: a.reshape(1, -1).astype(F32)
    lb_all = jnp.cumsum(jax.nn.softmax(hgrn_lb.astype(F32), axis=1), axis=1)
    rw = jnp.concatenate([router_group_w[0], router_expert_w[0]], axis=1).T
    rw = jnp.pad(rw, ((0, ROUTER_PAD - rw.shape[0]), (0, 0)))
    rw_hi = rw.astype(BF16)
    rw_lo = (rw - rw_hi.astype(F32)).astype(BF16)
    rb = jnp.pad(jnp.concatenate([router_group_b[0], router_expert_b[0]]), (0, ROUTER_PAD - N_GROUPS - N_EXPERTS))
    perm = jnp.arange(D_RWKV).reshape(RWKV_HEADS, RWKV_HEAD).T.reshape(-1)
    cols = jnp.concatenate([perm, C_K + perm, C_V + perm, jnp.arange(C_WD, C_RWKV_END)])
    col = lambda a: jnp.broadcast_to(a.astype(F32)[..., None], a.shape + (LANES,))
    head = jnp.arange(D_RWKV) % RWKV_HEADS
    return {
        "norm1_w": row(norm1_w[0]),
        "w_r": w_in[0][:, :C_RWKV_END].T[cols].astype(BF16),
        "w_h": w_in[0][:, C_RWKV_END:].astype(BF16),
        "mu": col(mu_shift[0][cols]),
        "w0": col(rwkv_w0[0][:, perm]), "w2": jnp.swapaxes(rwkv_w2[0], 1, 2)[:, perm].astype(BF16),
        "a0": col(rwkv_a0[0][:, perm]), "a2": jnp.swapaxes(rwkv_a2[0], 1, 2)[:, perm].astype(BF16),
        "g2": rwkv_g2[0][:, perm].astype(BF16),
        "k_k": col(rwkv_k_k[0][perm]), "k_a": col(rwkv_k_a[0][perm]),
        "r_k": col(rwkv_r_k[0].reshape(-1)[perm]),
        "lnx_w": row(rwkv_lnx_w[0][perm]), "lnx_b": row(rwkv_lnx_b[0][perm]),
        "lb": lb_all[:, 0],
        "gnorm_w": row(jnp.tile(hgrn_gnorm_w[0], HGRN_HEADS)),
        "hd64": (head[:, None] == head[None, :]).astype(BF16),
        "hd128": _block_diag_ones(D_HGRN, HGRN_HEAD),
        "w_out": jnp.concatenate([w_out[0][:D_RWKV][perm], w_out[0][D_RWKV:]]).astype(BF16),
        "norm2_w": row(norm2_w[0]),
        "rw_hi": rw_hi, "rw_lo": rw_lo, "rb": col(rb),
        "moe_wg": moe_w_gate[0].astype(BF16), "moe_wu": moe_w_up[0].astype(BF16),
        "moe_wd": moe_w_down[0].astype(BF16),
        "final_w": row(final_norm_w),
    }


def _forward(x, p):
    B, T, D = x.shape
    z, g, bv, q, lf, kf, iv, sg = _proj_call(x, p)

    chains = B * RWKV_HEADS
    vq = LANES // chains
    nv = RWKV_HEAD // (vq * SUBLANES)
    a, v = _to_chains_call(z, vq, nv * SUBLANES)
    ya = _from_chains_call(_rwkv_call(a, v, nv), B, vq, nv * SUBLANES)

    of, ob = _hgrn_call(q, iv, lf, kf)
    x1, h2, logits = _mix_call(x, ya, g, bv, of, ob, sg, p)
    N = B * T
    moe = _route_and_moe(h2.reshape(N, D), jnp.swapaxes(logits, 0, 1).reshape(ROUTER_PAD, N), p)
    return _final_call(x1.reshape(N, D), moe, p["final_w"]).reshape(B, T, D)


def kernel(x_prompt, x_sample, norm1_w, w_in, mu_shift, rwkv_w0, rwkv_w2, rwkv_a0, rwkv_a2, rwkv_g2, rwkv_k_k, rwkv_k_a, rwkv_r_k, rwkv_lnx_w, rwkv_lnx_b, hgrn_lb, hgrn_gnorm_w, w_out, norm2_w, router_group_w, router_group_b, router_expert_w, router_expert_b, moe_w_gate, moe_w_up, moe_w_down, final_norm_w):
    p = _prepare(norm1_w, w_in, mu_shift, rwkv_w0, rwkv_w2, rwkv_a0, rwkv_a2, rwkv_g2, rwkv_k_k, rwkv_k_a,
                 rwkv_r_k, rwkv_lnx_w, rwkv_lnx_b, hgrn_lb, hgrn_gnorm_w, w_out, norm2_w, router_group_w,
                 router_group_b, router_expert_w, router_expert_b, moe_w_gate, moe_w_up, moe_w_down,
                 final_norm_w)
    return (_forward(x_prompt, p), _forward(x_sample, p))
```

```python
import functools

import jax
import jax.numpy as jnp
from jax import lax
from jax.experimental import pallas as pl
from jax.experimental.pallas import tpu as pltpu

F32 = jnp.float32
BF16 = jnp.bfloat16

D_MODEL = 1024
D_RWKV = 512
RWKV_HEAD = 64
RWKV_HEADS = 8
D_HGRN = 512
HGRN_HEAD = 128
HGRN_HEADS = 4
DECAY_LORA = 64
AAA_LORA = 64
GATE_LORA = 128
N_GROUPS = 4
EXPERTS_PER_GROUP = 8
N_EXPERTS = 32
D_EXPERT = 512
NORM_EPS = 1e-6
HGRN_NORM_EPS = 1e-5
LNX_EPS = RWKV_HEAD * 1e-5

C_K = 512
C_V = 1024
C_WD = 1536
C_AD = 1664
C_GD = 1792
C_RWKV_END = 1920
D_IN = 4480
HG_Q = 0
HG_F = 512
HG_I = 1536
HG_G = 2048
D_HG_IN = 2560

SUBLANES = 8
LANES = 128
VMEM_LIMIT = 56 * 1024 * 1024

PROJ_ROWS = 256
HALO = 8
SCAN_STEPS = 64
HGRN_CHUNK = 64
HGRN_SUB = 8
LOG2E = 1.4426950408889634
HGRN_TILE = 512
MOE_ROWS = 256
ROUTER_PAD = 128
RELAYOUT_T = 128

Z_R, Z_KK, Z_W, Z_K, Z_B, Z_V = 0, 1, 2, 4, 6, 8
Z_COUNT = 9
SHIFT_R, SHIFT_W, SHIFT_K, SHIFT_B = 0, 1, 3, 2


def _dot(a, b):
    return jnp.dot(a, b, preferred_element_type=F32)


def _dot_nt(a, b):
    return lax.dot_general(a, b, (((1,), (1,)), ((), ())), preferred_element_type=F32)


def _dot_tn(a, b):
    return lax.dot_general(a, b, (((0,), (0,)), ((), ())), preferred_element_type=F32)


def _split2(a):
    hi = a.astype(BF16)
    lo = (a - hi.astype(F32)).astype(BF16)
    return hi, lo


def _split3(a):
    hi = a.astype(BF16)
    r1 = a - hi.astype(F32)
    mid = r1.astype(BF16)
    lo = (r1 - mid.astype(F32)).astype(BF16)
    return hi, mid, lo


def _dot_lhs2(a, b_bf16):
    hi, lo = _split2(a)
    return _dot(hi, b_bf16) + _dot(lo, b_bf16)


def _sigmoid(x):
    return 1.0 / (1.0 + jnp.exp(-x))


def _softplus(x):
    return jnp.maximum(x, 0.0) + jnp.log(1.0 + jnp.exp(-jnp.abs(x)))


def _lanes(c, n):
    if n == LANES:
        return c
    if n < LANES:
        return c[:, :n]
    return jnp.concatenate([c] * (n // LANES), axis=1)


def _dot_rhs2(a_bf16, b):
    hi, lo = _split2(b)
    return _dot(a_bf16, hi) + _dot(a_bf16, lo)


def _proj_kernel(x_ref, xp_ref, xn_ref, n1_ref, wr_ref, wh_ref, mu_ref, w0_ref, w2_ref, a0_ref, a2_ref,
                 g2_ref, kk_ref, ka_ref, rk_ref, lb_ref, hd_ref,
                 z_o, g_o, bv_o, q_o, lf_o, kf_o, iv_o, sg_o):
    i = pl.program_id(1)
    n_i = pl.num_programs(1)
    rows = x_ref.shape[1]

    def norm(xf):
        ms = jnp.mean(xf * xf, axis=-1, keepdims=True)
        return (xf * lax.rsqrt(ms + NORM_EPS) * n1_ref[...]).astype(BF16)

    h_c = norm(x_ref[0])
    h_h = norm(jnp.concatenate([xp_ref[0] * (i > 0).astype(F32), xn_ref[0] * (i < n_i - 1).astype(F32)], axis=0))

    wr = wr_ref[...]
    u_c = _dot_nt(wr, h_c)
    u_h = _dot_nt(wr, h_h)
    lane = lax.broadcasted_iota(jnp.int32, (1, rows), 1)
    u_prev = jnp.where(lane == 0, u_h[:, HALO - 1:HALO], pltpu.roll(u_c, 1, 1))
    u_next = jnp.where(lane == rows - 1, u_h[:, HALO:HALO + 1], pltpu.roll(u_c, rows - 1, 1))
    us = u_c + (0.5 * (u_prev + u_next) - u_c) * _lanes(mu_ref[...], rows)

    def put(j, a):
        z_o[j] = a.reshape(RWKV_HEAD, SUBLANES, rows)

    r = us[0:C_K]
    k = us[C_K:C_V]
    v = us[C_V:C_WD]
    hd = hd_ref[...]
    kk = k * _lanes(kk_ref[...], rows)
    kn = kk / jnp.maximum(jnp.sqrt(_dot_rhs2(hd, kk * kk)), 1e-12)
    gd = us[C_GD:C_RWKV_END]
    g_o[0] = _dot_tn(_sigmoid(gd).astype(BF16), g2_ref[...])
    put(Z_R, r)
    put(Z_KK, kn)
    put(Z_V, v)
    kd_sum = jnp.zeros_like(k)
    for d in range(2):
        wd = us[C_WD + d * DECAY_LORA:C_WD + (d + 1) * DECAY_LORA]
        ad = us[C_AD + d * AAA_LORA:C_AD + (d + 1) * AAA_LORA]
        zw = _lanes(w0_ref[d], rows) + _dot(w2_ref[d], jnp.tanh(wd).astype(BF16))
        w_log = -_softplus(-zw) - 0.5
        put(Z_W + d, jnp.exp(-jnp.exp(w_log)))
        a = _sigmoid(_lanes(a0_ref[d], rows) + _dot(a2_ref[d], ad.astype(BF16)))
        kd = k * (1.0 + (a - 1.0) * _lanes(ka_ref[...], rows))
        put(Z_K + d, kd)
        put(Z_B + d, kn * a)
        kd_sum = kd_sum + kd
    bonus = _dot_rhs2(hd, r * kd_sum * _lanes(rk_ref[...], rows))
    bv_o[0] = (bonus * v).T

    uh = _dot(h_c, wh_ref[...])
    uq = uh[:, HG_Q:HG_F]
    q_o[0] = uq * _sigmoid(uq)
    iv_o[0] = uh[:, HG_I:HG_G]
    ug = uh[:, HG_G:D_HG_IN]
    sg_o[0] = ug * _sigmoid(ug)
    for d in range(2):
        fr = uh[:, HG_F + d * D_HGRN:HG_F + (d + 1) * D_HGRN]
        lb = lb_ref[d:d + 1, :]
        f = lb + (1.0 - lb) * _sigmoid(fr)
        lf_o[d, 0] = jnp.log(f)
        kf_o[d, 0] = 1.0 - f


def _proj_call(x, p):
    B, T, D = x.shape
    rows = min(PROJ_ROWS, T)
    n_i = T // rows
    rb = rows // HALO

    def full(a):
        nd = a.ndim
        return pl.BlockSpec(a.shape, lambda b, i, _n=nd: (0,) * _n)

    tile = pl.BlockSpec((1, rows, D), lambda b, i: (b, i, 0))
    prev = pl.BlockSpec((1, HALO, D), lambda b, i: (b, jnp.maximum(i * rb - 1, 0), 0))
    nxt = pl.BlockSpec((1, HALO, D), lambda b, i: (b, jnp.minimum((i + 1) * rb, T // HALO - 1), 0))
    consts = [p["norm1_w"], p["w_r"], p["w_h"], p["mu"], p["w0"], p["w2"], p["a0"], p["a2"], p["g2"],
              p["k_k"], p["k_a"], p["r_k"], p["lb"], p["hd64"]]
    one = jax.ShapeDtypeStruct((B, T, D_RWKV), F32)
    two = jax.ShapeDtypeStruct((2, B, T, D_RWKV), F32)
    one_spec = pl.BlockSpec((1, rows, D_RWKV), lambda b, i: (b, i, 0))
    two_spec = pl.BlockSpec((2, 1, rows, D_RWKV), lambda b, i: (0, b, i, 0))
    z = jax.ShapeDtypeStruct((Z_COUNT, RWKV_HEAD, B * RWKV_HEADS, T), F32)
    z_spec = pl.BlockSpec((Z_COUNT, RWKV_HEAD, RWKV_HEADS, rows), lambda b, i: (0, 0, b, i))
    kinds = [z, one, one, one, two, two, one, one]
    return pl.pallas_call(
        _proj_kernel,
        out_shape=kinds,
        grid=(B, n_i),
        in_specs=[tile, prev, nxt] + [full(a) for a in consts],
        out_specs=[z_spec if s is z else (one_spec if s is one else two_spec) for s in kinds],
        compiler_params=pltpu.CompilerParams(
            dimension_semantics=("parallel", "arbitrary"), vmem_limit_bytes=VMEM_LIMIT),
        name="proj",
    )(x, x, x, *consts)


def _rwkv_kernel(r_ref, kk_ref, v_ref, w_ref, k_ref, b_ref, y_ref, s_ref, *, nv, steps):
    d = pl.program_id(0)
    i = pl.program_id(1)
    K = RWKV_HEAD
    n_par = max(1, SUBLANES // nv)

    @pl.when(i == 0)
    def _():
        s_ref[...] = jnp.zeros_like(s_ref)

    def row(ref, tt, k):
        return jnp.broadcast_to(ref[tt[0], k, pl.ds(tt[1], 1), :], (SUBLANES, LANES))

    def tree_sum(parts):
        parts = [x for x in parts if x is not None]
        while len(parts) > 1:
            parts = [parts[j] + parts[j + 1] if j + 1 < len(parts) else parts[j]
                     for j in range(0, len(parts), 2)]
        return parts[0]

    def step(j, carry):
        t = jnp.where(d == 0, j, steps - 1 - j)
        tt = (t // SUBLANES, t % SUBLANES)
        acc = [[None] * n_par for _ in range(nv)]
        for k in range(K):
            kk_row = row(kk_ref, tt, (k + SHIFT_W) % K)
            for vg in range(nv):
                term = s_ref[vg * K + k] * kk_row
                slot = k % n_par
                acc[vg][slot] = term if acc[vg][slot] is None else acc[vg][slot] + term
        sa = [-tree_sum(acc[vg]) for vg in range(nv)]
        val = [v_ref[t, pl.ds(vg * SUBLANES, SUBLANES), :] for vg in range(nv)]
        yacc = [[None] * n_par for _ in range(nv)]
        for k in range(K):
            w_row = row(w_ref, tt, (k + SHIFT_W) % K)
            b_row = row(b_ref, tt, (k + SHIFT_B) % K)
            k_row = row(k_ref, tt, (k + SHIFT_K) % K)
            r_row = row(r_ref, tt, (k + SHIFT_R) % K)
            for vg in range(nv):
                s_new = s_ref[vg * K + k] * w_row + sa[vg] * b_row + val[vg] * k_row
                s_ref[vg * K + k] = s_new
                term = s_new * r_row
                slot = k % n_par
                yacc[vg][slot] = term if yacc[vg][slot] is None else yacc[vg][slot] + term
        for vg in range(nv):
            y_ref[t, pl.ds(vg * SUBLANES, SUBLANES), :] = tree_sum(yacc[vg])
        return carry

    lax.fori_loop(0, steps, step, 0)


def _rwkv_call(a, v, nv):
    T = a.shape[1] * SUBLANES
    steps = min(SCAN_STEPS, T)
    n_t = T // steps
    nvp = nv * SUBLANES

    def tblk(d, i):
        return jnp.where(d == 0, i, n_t - 1 - i)

    kblock = (None, steps // SUBLANES, RWKV_HEAD, SUBLANES, LANES)

    def shared(j):
        return pl.BlockSpec(kblock, lambda d, i: (j, tblk(d, i), 0, 0, 0))

    def perdir(j):
        return pl.BlockSpec(kblock, lambda d, i: (j + d, tblk(d, i), 0, 0, 0))

    vspec = pl.BlockSpec((steps, nvp, LANES), lambda d, i: (tblk(d, i), 0, 0))
    yspec = pl.BlockSpec((None, steps, nvp, LANES), lambda d, i: (d, tblk(d, i), 0, 0))
    return pl.pallas_call(
        functools.partial(_rwkv_kernel, nv=nv, steps=steps),
        out_shape=jax.ShapeDtypeStruct((2, T, nvp, LANES), F32),
        grid=(2, n_t),
        in_specs=[shared(Z_R), shared(Z_KK), vspec, perdir(Z_W), perdir(Z_K), perdir(Z_B)],
        out_specs=yspec,
        scratch_shapes=[pltpu.VMEM((nv * RWKV_HEAD, SUBLANES, LANES), F32)],
        compiler_params=pltpu.CompilerParams(
            dimension_semantics=("parallel", "arbitrary"), vmem_limit_bytes=VMEM_LIMIT),
        name="rwkv",
    )(a, a, v, a, a, a)


def _plane_shift(j):
    return jnp.where(j < Z_KK, SHIFT_R, jnp.where(j < Z_K, SHIFT_W, jnp.where(j < Z_B, SHIFT_K, SHIFT_B)))


def _to_chains_k_kernel(z_ref, o_ref, *, vq):
    tl = z_ref.shape[2]
    shift = _plane_shift(pl.program_id(0))
    for p in range(RWKV_HEAD):
        a = z_ref[p]
        a = (jnp.concatenate([a] * vq, axis=0) if vq > 1 else a).T
        o_ref[:, (p + shift) % RWKV_HEAD] = a.reshape(tl // SUBLANES, SUBLANES, LANES)


def _to_chains_v_kernel(z_ref, o_ref, *, vq, nvp):
    for p in range(nvp):
        o_ref[:, p, :] = jnp.concatenate([z_ref[q * nvp + p] for q in range(vq)], axis=0).T


def _to_chains_call(z, vq, nvp):
    n_arr, planes, C, T = z.shape
    tl = min(RELAYOUT_T, T)
    a = pl.pallas_call(
        functools.partial(_to_chains_k_kernel, vq=vq),
        out_shape=jax.ShapeDtypeStruct((Z_V, T // SUBLANES, planes, SUBLANES, LANES), F32),
        grid=(Z_V, T // tl),
        in_specs=[pl.BlockSpec((None, planes, C, tl), lambda j, t: (j, 0, 0, t))],
        out_specs=pl.BlockSpec((None, tl // SUBLANES, planes, SUBLANES, LANES), lambda j, t: (j, t, 0, 0, 0)),
        compiler_params=pltpu.CompilerParams(
            dimension_semantics=("parallel", "parallel"), vmem_limit_bytes=VMEM_LIMIT),
        name="to_chains_k",
    )(z)
    v = pl.pallas_call(
        functools.partial(_to_chains_v_kernel, vq=vq, nvp=nvp),
        out_shape=jax.ShapeDtypeStruct((T, nvp, LANES), F32),
        grid=(T // tl,),
        in_specs=[pl.BlockSpec((None, planes, C, tl), lambda t: (Z_V, 0, 0, t))],
        out_specs=pl.BlockSpec((tl, nvp, LANES), lambda t: (t, 0, 0)),
        compiler_params=pltpu.CompilerParams(
            dimension_semantics=("parallel",), vmem_limit_bytes=VMEM_LIMIT),
        name="to_chains_v",
    )(z)
    return a, v


def _from_chains_kernel(y_ref, o_ref, zs, *, vq, nvp):
    B = o_ref.shape[0]
    C = B * RWKV_HEADS
    tl = o_ref.shape[1]
    for p in range(nvp):
        zs[p] = (y_ref[0, :, p, :] + y_ref[1, :, p, :]).T
    for b in range(B):
        rows = [zs[:, pl.ds(q * C + b * RWKV_HEADS, RWKV_HEADS), :] for q in range(vq)]
        slab = jnp.concatenate(rows, axis=0) if vq > 1 else rows[0]
        o_ref[b] = slab.reshape(D_RWKV, tl).T


def _from_chains_call(y, B, vq, nvp):
    T = y.shape[1]
    tl = min(RELAYOUT_T, T)
    return pl.pallas_call(
        functools.partial(_from_chains_kernel, vq=vq, nvp=nvp),
        out_shape=jax.ShapeDtypeStruct((B, T, D_RWKV), F32),
        grid=(T // tl,),
        in_specs=[pl.BlockSpec((2, tl, nvp, LANES), lambda t: (0, t, 0, 0))],
        out_specs=pl.BlockSpec((B, tl, D_RWKV), lambda t: (0, t, 0)),
        scratch_shapes=[pltpu.VMEM((nvp, LANES, tl), F32)],
        compiler_params=pltpu.CompilerParams(
            dimension_semantics=("parallel",), vmem_limit_bytes=VMEM_LIMIT),
        name="from_chains",
    )(y)


def _hgrn_direction(q, kf, val, lf, st_ref, tri, off_mask, sel, ones, rev, sub):
    C = q.shape[0]
    l_hi, l_mid, l_lo = _split3(lf)
    G = (_dot(tri, l_hi) + _dot(tri, l_mid) + _dot(tri, l_lo)) * LOG2E
    last = 0 if rev else C - 1
    g_tot = G[last:last + 1, :]
    outs = []
    for h in range(HGRN_HEADS):
        sl = slice(h * HGRN_HEAD, (h + 1) * HGRN_HEAD)
        Gh, qh, kh, vh = G[:, sl], q[:, sl], kf[:, sl], val[:, sl]
        st = st_ref[h]
        o = _dot_nt((qh * jnp.exp2(Gh)).astype(BF16), st.astype(BF16))
        a_off = None
        for lv, L in enumerate(_hgrn_levels(C, sub)):
            mid = L // 2 if rev else L // 2 - 1
            refs = [jnp.broadcast_to(Gh[m * L + mid:m * L + mid + 1, :], (L, HGRN_HEAD)) for m in range(C // L)]
            ref = jnp.concatenate(refs, axis=0) if len(refs) > 1 else refs[0]
            ql = qh * jnp.exp2(jnp.minimum(Gh - ref, 0.0))
            kl = kh * jnp.exp2(jnp.minimum(ref - Gh, 0.0))
            al = _dot_nt(ql.astype(BF16), kl.astype(BF16)) * off_mask[lv]
            a_off = al if a_off is None else a_off + al
        if a_off is not None:
            o = o + _dot(a_off.astype(BF16), vh.astype(BF16))
        parts = []
        for t in range(C):
            b0 = (t // sub) * sub
            rel = jnp.exp2(jnp.minimum(Gh[t:t + 1, :] - Gh[b0:b0 + sub, :], 0.0))
            parts.append(rel * kh[b0:b0 + sub, :] * qh[t:t + 1, :])
        ab = _dot(jnp.concatenate(parts, axis=0).astype(BF16), ones)
        wv = jnp.concatenate(
            [ab[t * sub:(t + 1) * sub] * vh[(t // sub) * sub:(t // sub + 1) * sub] for t in range(C)], axis=0)
        o = o + _dot(sel, wv.astype(BF16))
        outs.append(o)
        kdec = kh * jnp.exp2(g_tot[:, sl] - Gh)
        st_ref[h] = st * jnp.exp2(g_tot[:, sl]) + _dot_tn(vh.astype(BF16), kdec.astype(BF16))
    return jnp.concatenate(outs, axis=-1)


def _hgrn_kernel(qf_ref, vf_ref, lff_ref, kff_ref, qb_ref, vb_ref, lfb_ref, kfb_ref,
                 trif_ref, trib_ref, offf_ref, offb_ref, self_ref, selb_ref, ones_ref,
                 of_ref, ob_ref, stf_ref, stb_ref, *, chunk, sub):
    i = pl.program_id(1)
    n_c = qf_ref.shape[1] // chunk

    @pl.when(i == 0)
    def _():
        stf_ref[...] = jnp.zeros_like(stf_ref)
        stb_ref[...] = jnp.zeros_like(stb_ref)

    def body(c, carry):
        cf = pl.multiple_of(c * chunk, chunk)
        cb = pl.multiple_of((n_c - 1 - c) * chunk, chunk)
        of_ref[0, pl.ds(cf, chunk), :] = _hgrn_direction(
            qf_ref[0, pl.ds(cf, chunk), :], kff_ref[0, pl.ds(cf, chunk), :], vf_ref[0, pl.ds(cf, chunk), :],
            lff_ref[0, pl.ds(cf, chunk), :], stf_ref, trif_ref[...], offf_ref[...], self_ref[...], ones_ref[...],
            rev=False, sub=sub)
        ob_ref[0, pl.ds(cb, chunk), :] = _hgrn_direction(
            qb_ref[0, pl.ds(cb, chunk), :], kfb_ref[0, pl.ds(cb, chunk), :], vb_ref[0, pl.ds(cb, chunk), :],
            lfb_ref[0, pl.ds(cb, chunk), :], stb_ref, trib_ref[...], offb_ref[...], selb_ref[...], ones_ref[...],
            rev=True, sub=sub)
        return carry

    lax.fori_loop(0, n_c, body, 0)


def _hgrn_levels(chunk, sub):
    levels = []
    L = 2 * sub
    while L <= chunk:
        levels.append(L)
        L *= 2
    return levels


def _hgrn_consts(chunk, sub):
    t = jnp.arange(chunk)
    tri_f = (t[None, :] <= t[:, None])
    off_f = jnp.stack([(t[:, None] // L == t[None, :] // L) & (t[:, None] % L >= L // 2) & (t[None, :] % L < L // 2)
                       for L in _hgrn_levels(chunk, sub)] or [jnp.zeros((chunk, chunk), bool)])
    col = jnp.arange(chunk * sub)
    same_t = (col // sub)[None, :] == t[:, None]
    j = (col % sub)[None, :]
    sel_f = same_t & (j <= (t % sub)[:, None])
    sel_b = same_t & (j >= (t % sub)[:, None])
    return [tri_f.astype(BF16), tri_f.T.astype(BF16), off_f.astype(F32), jnp.swapaxes(off_f, 1, 2).astype(F32),
            sel_f.astype(BF16), sel_b.astype(BF16), jnp.ones((HGRN_HEAD, HGRN_HEAD), BF16)]


def _hgrn_call(q, iv, lf, kf):
    B, T, _ = q.shape
    tile = min(HGRN_TILE, T)
    chunk = min(HGRN_CHUNK, tile)
    sub = min(HGRN_SUB, chunk)
    n_i = T // tile
    consts = _hgrn_consts(chunk, sub)
    fwd = pl.BlockSpec((1, tile, D_HGRN), lambda b, i: (b, i, 0))
    bwd = pl.BlockSpec((1, tile, D_HGRN), lambda b, i: (b, n_i - 1 - i, 0))
    fwd2 = pl.BlockSpec((None, 1, tile, D_HGRN), lambda b, i: (0, b, i, 0))
    bwd2 = pl.BlockSpec((None, 1, tile, D_HGRN), lambda b, i: (1, b, n_i - 1 - i, 0))
    full = [pl.BlockSpec(a.shape, lambda b, i, _n=a.ndim: (0,) * _n) for a in consts]
    out = jax.ShapeDtypeStruct((B, T, D_HGRN), F32)
    state = pltpu.VMEM((HGRN_HEADS, HGRN_HEAD, HGRN_HEAD), F32)
    return pl.pallas_call(
        functools.partial(_hgrn_kernel, chunk=chunk, sub=sub),
        out_shape=[out, out],
        grid=(B, n_i),
        in_specs=[fwd, fwd, fwd2, fwd2, bwd, bwd, bwd2, bwd2] + full,
        out_specs=[fwd, bwd],
        scratch_shapes=[state, state],
        compiler_params=pltpu.CompilerParams(
            dimension_semantics=("parallel", "arbitrary"), vmem_limit_bytes=VMEM_LIMIT),
        name="hgrn",
    )(q, iv, lf, kf, q, iv, lf, kf, *consts)


def _mix_kernel(x_ref, ya_ref, g_ref, bv_ref, of_ref, ob_ref, sg_ref, lnw_ref, lnb_ref, gnw_ref,
                hd64_ref, hd128_ref, wo_ref, n2_ref, rwh_ref, rwl_ref, rb_ref,
                x1_o, h2_o, lg_o):
    hd64 = hd64_ref[...]
    y = ya_ref[0]
    inv = 1.0 / RWKV_HEAD
    mean = _dot_lhs2(y, hd64) * inv
    yc = y - mean
    var = _dot_lhs2(yc * yc, hd64) * inv
    ya = (yc * lax.rsqrt(var + LNX_EPS)) * lnw_ref[...] + lnb_ref[...]
    ya = (ya + bv_ref[0]) * g_ref[0]
    o = of_ref[0] + ob_ref[0]
    ms = _dot_lhs2(o * o, hd128_ref[...]) * (1.0 / HGRN_HEAD)
    yb = o * lax.rsqrt(ms + HGRN_NORM_EPS) * gnw_ref[...] * sg_ref[0]
    mixed = jnp.concatenate([ya, yb], axis=-1).astype(BF16)
    x1 = x_ref[0] + _dot(mixed, wo_ref[...])
    x1_o[0] = x1
    ms2 = jnp.mean(x1 * x1, axis=-1, keepdims=True)
    h2 = x1 * lax.rsqrt(ms2 + NORM_EPS) * n2_ref[...]
    h2_o[0] = h2
    h_hi, h_lo = _split2(h2)
    w_hi = rwh_ref[...]
    lg_o[0] = (_dot_nt(w_hi, h_hi) + _dot_nt(w_hi, h_lo) + _dot_nt(rwl_ref[...], h_hi)
               + _lanes(rb_ref[...], h2.shape[0]))


def _mix_call(x, ya, g, bv, of, ob, sg, p):
    B, T, D = x.shape
    rows = min(PROJ_ROWS, T)

    def full(a):
        nd = a.ndim
        return pl.BlockSpec(a.shape, lambda b, i, _n=nd: (0,) * _n)

    def tile(w):
        return pl.BlockSpec((1, rows, w), lambda b, i: (b, i, 0))

    consts = [p["lnx_w"], p["lnx_b"], p["gnorm_w"], p["hd64"], p["hd128"], p["w_out"], p["norm2_w"],
              p["rw_hi"], p["rw_lo"], p["rb"]]
    return pl.pallas_call(
        _mix_kernel,
        out_shape=[jax.ShapeDtypeStruct((B, T, D), F32), jax.ShapeDtypeStruct((B, T, D), F32),
                   jax.ShapeDtypeStruct((B, ROUTER_PAD, T), F32)],
        grid=(B, T // rows),
        in_specs=[tile(D)] + [tile(D_RWKV)] * 6 + [full(a) for a in consts],
        out_specs=[tile(D), tile(D), pl.BlockSpec((1, ROUTER_PAD, rows), lambda b, i: (b, 0, i))],
        compiler_params=pltpu.CompilerParams(
            dimension_semantics=("parallel", "parallel"), vmem_limit_bytes=VMEM_LIMIT),
        name="mix",
    )(x, ya, g, bv, of, ob, sg, *consts)


def _moe_kernel(be_ref, nu_ref, tok0_ref, tokn_ref, dst_ref, gt_ref, h_hbm, wg_ref, wu_ref, wd_ref, y_hbm,
                xbuf, ybuf, gsem, ssem):
    del be_ref
    i = pl.program_id(0)
    nb_used = nu_ref[0]
    slot = lax.rem(i, 2)

    def gather_copy(tok, r, s):
        return pltpu.make_async_copy(h_hbm.at[pl.ds(tok, 1)], xbuf.at[s, pl.ds(r, 1)], gsem.at[s])

    def scatter_copy(dst, r, s):
        return pltpu.make_async_copy(ybuf.at[s, pl.ds(r, 1)], y_hbm.at[pl.ds(dst, 1)], ssem.at[s])

    def start_gather(tok_ref, s):
        def body(r, c):
            gather_copy(tok_ref[0, 0, r], r, s).start()
            return c
        lax.fori_loop(0, MOE_ROWS, body, 0, unroll=8)

    def start_scatter(s):
        def body(r, c):
            scatter_copy(dst_ref[0, 0, r], r, s).start()
            return c
        lax.fori_loop(0, MOE_ROWS, body, 0, unroll=8)

    def wait_gather(s):
        pltpu.make_async_copy(h_hbm.at[pl.ds(0, MOE_ROWS)], xbuf.at[s], gsem.at[s]).wait()

    def wait_scatter(s):
        pltpu.make_async_copy(ybuf.at[s], y_hbm.at[pl.ds(0, MOE_ROWS)], ssem.at[s]).wait()

    @pl.when(i == 0)
    def _():
        plane = y_hbm.shape[0] // 2
        ybuf[...] = jnp.zeros_like(ybuf)
        for s in range(2):
            spare = pltpu.make_async_copy(
                ybuf.at[s], y_hbm.at[pl.ds((s + 1) * plane - MOE_ROWS, MOE_ROWS)], ssem.at[s])
            spare.start()
            spare.wait()

    @pl.when(jnp.logical_and(i == 0, nb_used > 0))
    def _():
        start_gather(tok0_ref, 0)

    @pl.when(i + 1 < nb_used)
    def _():
        start_gather(tokn_ref, 1 - slot)

    @pl.when(i < nb_used)
    def _():
        wait_gather(slot)

        @pl.when(i >= 2)
        def _():
            wait_scatter(slot)

        x = xbuf[slot].astype(BF16)
        a = _dot(x, wg_ref[0])
        u = _dot(x, wu_ref[0])
        hid = (a * _sigmoid(a)) * u
        ybuf[slot] = _dot(hid.astype(BF16), wd_ref[0]) * gt_ref[...]
        start_scatter(slot)

        @pl.when(i == nb_used - 1)
        def _():
            wait_scatter(slot)

            @pl.when(i >= 1)
            def _():
                wait_scatter(1 - slot)


def _moe_call(block_e, nb_used, src_tok, dst_row, gate_slot, h2, p):
    N, D = h2.shape
    P = src_tok.shape[0]
    n_blocks = P // MOE_ROWS
    smem_blk = lambda f: pl.BlockSpec((1, 1, MOE_ROWS), f, memory_space=pltpu.SMEM)
    grid_spec = pltpu.PrefetchScalarGridSpec(
        num_scalar_prefetch=2,
        grid=(n_blocks,),
        in_specs=[
            smem_blk(lambda i, be, nu: (0, 0, 0)),
            smem_blk(lambda i, be, nu: (jnp.minimum(i + 1, n_blocks - 1), 0, 0)),
            smem_blk(lambda i, be, nu: (i, 0, 0)),
            pl.BlockSpec((MOE_ROWS, 1), lambda i, be, nu: (i, 0)),
            pl.BlockSpec(memory_space=pl.ANY),
            pl.BlockSpec((1, D, D_EXPERT), lambda i, be, nu: (be[i], 0, 0)),
            pl.BlockSpec((1, D, D_EXPERT), lambda i, be, nu: (be[i], 0, 0)),
            pl.BlockSpec((1, D_EXPERT, D), lambda i, be, nu: (be[i], 0, 0)),
        ],
        out_specs=pl.BlockSpec(memory_space=pl.ANY),
        scratch_shapes=[pltpu.VMEM((2, MOE_ROWS, D), F32), pltpu.VMEM((2, MOE_ROWS, D), F32),
                        pltpu.SemaphoreType.DMA((2,)), pltpu.SemaphoreType.DMA((2,))],
    )
    tok3 = src_tok.reshape(n_blocks, 1, MOE_ROWS)
    return pl.pallas_call(
        _moe_kernel,
        out_shape=jax.ShapeDtypeStruct((2 * (N + MOE_ROWS), D), F32),
        grid_spec=grid_spec,
        compiler_params=pltpu.CompilerParams(
            dimension_semantics=("arbitrary",), vmem_limit_bytes=VMEM_LIMIT),
        name="moe",
    )(block_e, nb_used, tok3, tok3, dst_row.reshape(n_blocks, 1, MOE_ROWS), gate_slot, h2,
      p["moe_wg"], p["moe_wu"], p["moe_wd"])


def _route_and_moe(h2, logits, p):
    N, D = h2.shape
    glog = logits[:N_GROUPS]
    elog = logits[N_GROUPS:N_GROUPS + N_EXPERTS].reshape(N_GROUPS, EXPERTS_PER_GROUP, N)
    grp = jnp.argmax(glog, axis=0)
    p_grp = 1.0 / jnp.sum(jnp.exp(glog - jnp.max(glog, axis=0, keepdims=True)), axis=0)
    gsel = jnp.arange(N_GROUPS)[:, None, None] == grp[None, None, :]
    elog_g = jnp.sum(jnp.where(gsel, elog, 0.0), axis=0)
    i1 = jnp.argmax(elog_g, axis=0)
    v1 = jnp.max(elog_g, axis=0)
    rest = jnp.where(jnp.arange(EXPERTS_PER_GROUP)[:, None] == i1[None, :], -jnp.inf, elog_g)
    i2 = jnp.argmax(rest, axis=0)
    e21 = jnp.exp(jnp.max(rest, axis=0) - v1)
    gate = jnp.stack([p_grp / (1.0 + e21), p_grp * e21 / (1.0 + e21)], axis=1).reshape(2 * N)
    eid = (grp[:, None] * EXPERTS_PER_GROUP + jnp.stack([i1, i2], axis=1)).reshape(2 * N).astype(jnp.int32)

    M = 2 * N
    order = jnp.argsort(eid, stable=True).astype(jnp.int32)
    experts = jnp.arange(N_EXPERTS, dtype=jnp.int32)
    e_sorted = eid[order]
    starts = jnp.searchsorted(e_sorted, experts, side="left").astype(jnp.int32)
    counts = jnp.searchsorted(e_sorted, experts, side="right").astype(jnp.int32) - starts
    padded = (counts + MOE_ROWS - 1) // MOE_ROWS * MOE_ROWS
    pad_ends = jnp.cumsum(padded)
    pad_starts = pad_ends - padded
    n_blocks = (M + N_EXPERTS * (MOE_ROWS - 1) + MOE_ROWS - 1) // MOE_ROWS
    P = n_blocks * MOE_ROWS
    block_e = jnp.minimum(
        jnp.sum(pad_ends[None, :] <= (jnp.arange(n_blocks, dtype=jnp.int32) * MOE_ROWS)[:, None], axis=1),
        N_EXPERTS - 1).astype(jnp.int32)
    slot = jnp.arange(P, dtype=jnp.int32)
    off = (slot.reshape(n_blocks, MOE_ROWS) - pad_starts[block_e][:, None])
    valid = ((off >= 0) & (off < counts[block_e][:, None])).reshape(P)
    src = jnp.clip(starts[block_e][:, None] + off, 0, M - 1).reshape(P)
    assign = order[src]
    gate_slot = jnp.where(valid, gate[assign], 0.0)[:, None]
    plane = N + MOE_ROWS
    src_tok = jnp.where(valid, assign // 2, 0)
    spare = ((slot // MOE_ROWS) % 2) * plane + N + slot % MOE_ROWS
    dst_row = jnp.where(valid, (assign % 2) * plane + assign // 2, spare)
    nb_used = (pad_ends[-1] // MOE_ROWS).astype(jnp.int32).reshape(1)
    return _moe_call(block_e, nb_used, src_tok, dst_row, gate_slot, h2, p)


def _final_kernel(x_ref, m0_ref, m1_ref, w_ref, o_ref):
    x = x_ref[...] + m0_ref[...] + m1_ref[...]
    ms = jnp.mean(x * x, axis=-1, keepdims=True)
    o_ref[...] = x * lax.rsqrt(ms + NORM_EPS) * w_ref[...]


def _final_call(x1, moe2, w):
    N, D = x1.shape
    rows = min(PROJ_ROWS * 2, N)
    tile = pl.BlockSpec((rows, D), lambda i: (i, 0))
    moe = moe2.reshape(2, moe2.shape[0] // 2, D)
    return pl.pallas_call(
        _final_kernel,
        out_shape=jax.ShapeDtypeStruct((N, D), F32),
        grid=(N // rows,),
        in_specs=[tile, pl.BlockSpec((None, rows, D), lambda i: (0, i, 0)),
                  pl.BlockSpec((None, rows, D), lambda i: (1, i, 0)), pl.BlockSpec((1, D), lambda i: (0, 0))],
        out_specs=tile,
        compiler_params=pltpu.CompilerParams(
            dimension_semantics=("parallel",), vmem_limit_bytes=VMEM_LIMIT),
        name="final",
    )(x1, moe, moe, w)


def _block_diag_ones(width, head):
    idx = jnp.arange(width) // head
    return (idx[:, None] == idx[None, :]).astype(BF16)


def _prepare(norm1_w, w_in, mu_shift, rwkv_w0, rwkv_w2, rwkv_a0, rwkv_a2, rwkv_g2, rwkv_k_k, rwkv_k_a,
             rwkv_r_k, rwkv_lnx_w, rwkv_lnx_b, hgrn_lb, hgrn_gnorm_w, w_out, norm2_w, router_group_w,
             router_group_b, router_expert_w, router_expert_b, moe_w_gate, moe_w_up, moe_w_down, final_norm_w):
    row = lambda a: a.reshape(1, -1).astype(F32)
    lb_all = jnp.cumsum(jax.nn.softmax(hgrn_lb.astype(F32), axis=1), axis=1)
    rw = jnp.concatenate([router_group_w[0], router_expert_w[0]], axis=1).T
    rw = jnp.pad(rw, ((0, ROUTER_PAD - rw.shape[0]), (0, 0)))
    rw_hi = rw.astype(BF16)
    rw_lo = (rw - rw_hi.astype(F32)).astype(BF16)
    rb = jnp.pad(jnp.concatenate([router_group_b[0], router_expert_b[0]]), (0, ROUTER_PAD - N_GROUPS - N_EXPERTS))
    perm = jnp.arange(D_RWKV).reshape(RWKV_HEADS, RWKV_HEAD).T.reshape(-1)
    cols = jnp.concatenate([perm, C_K + perm, C_V + perm, jnp.arange(C_WD, C_RWKV_END)])
    col = lambda a: jnp.broadcast_to(a.astype(F32)[..., None], a.shape + (LANES,))
    head = jnp.arange(D_RWKV) % RWKV_HEADS
    return {
        "norm1_w": row(norm1_w[0]),
        "w_r": w_in[0][:, :C_RWKV_END].T[cols].astype(BF16),
        "w_h": w_in[0][:, C_RWKV_END:].astype(BF16),
        "mu": col(mu_shift[0][cols]),
        "w0": col(rwkv_w0[0][:, perm]), "w2": jnp.swapaxes(rwkv_w2[0], 1, 2)[:, perm].astype(BF16),
        "a0": col(rwkv_a0[0][:, perm]), "a2": jnp.swapaxes(rwkv_a2[0], 1, 2)[:, perm].astype(BF16),
        "g2": rwkv_g2[0][:, perm].astype(BF16),
        "k_k": col(rwkv_k_k[0][perm]), "k_a": col(rwkv_k_a[0][perm]),
        "r_k": col(rwkv_r_k[0].reshape(-1)[perm]),
        "lnx_w": row(rwkv_lnx_w[0][perm]), "lnx_b": row(rwkv_lnx_b[0][perm]),
        "lb": lb_all[:, 0],
        "gnorm_w": row(jnp.tile(hgrn_gnorm_w[0], HGRN_HEADS)),
        "hd64": (head[:, None] == head[None, :]).astype(BF16),
        "hd128": _block_diag_ones(D_HGRN, HGRN_HEAD),
        "w_out": jnp.concatenate([w_out[0][:D_RWKV][perm], w_out[0][D_RWKV:]]).astype(BF16),
        "norm2_w": row(norm2_w[0]),
        "rw_hi": rw_hi, "rw_lo": rw_lo, "rb": col(rb),
        "moe_wg": moe_w_gate[0].astype(BF16), "moe_wu": moe_w_up[0].astype(BF16),
        "moe_wd": moe_w_down[0].astype(BF16),
        "final_w": row(final_norm_w),
    }


def _forward(x, p):
    B, T, D = x.shape
    z, g, bv, q, lf, kf, iv, sg = _proj_call(x, p)

    chains = B * RWKV_HEADS
    vq = LANES // chains
    nv = RWKV_HEAD // (vq * SUBLANES)
    a, v = _to_chains_call(z, vq, nv * SUBLANES)
    ya = _from_chains_call(_rwkv_call(a, v, nv), B, vq, nv * SUBLANES)

    of, ob = _hgrn_call(q, iv, lf, kf)
    x1, h2, logits = _mix_call(x, ya, g, bv, of, ob, sg, p)
    N = B * T
    moe = _route_and_moe(h2.reshape(N, D), jnp.swapaxes(logits, 0, 1).reshape(ROUTER_PAD, N), p)
    return _final_call(x1.reshape(N, D), moe, p["final_w"]).reshape(B, T, D)


def kernel(x_prompt, x_sample, norm1_w, w_in, mu_shift, rwkv_w0, rwkv_w2, rwkv_a0, rwkv_a2, rwkv_g2, rwkv_k_k, rwkv_k_a, rwkv_r_k, rwkv_lnx_w, rwkv_lnx_b, hgrn_lb, hgrn_gnorm_w, w_out, norm2_w, router_group_w, router_group_b, router_expert_w, router_expert_b, moe_w_gate, moe_w_up, moe_w_down, final_norm_w):
    p = _prepare(norm1_w, w_in, mu_shift, rwkv_w0, rwkv_w2, rwkv_a0, rwkv_a2, rwkv_g2, rwkv_k_k, rwkv_k_a,
                 rwkv_r_k, rwkv_lnx_w, rwkv_lnx_b, hgrn_lb, hgrn_gnorm_w, w_out, norm2_w, router_group_w,
                 router_group_b, router_expert_w, router_expert_b, moe_w_gate, moe_w_up, moe_w_down,
                 final_norm_w)
    return (_forward(x_prompt, p), _forward(x_sample, p))
```

```python
import functools

import jax
import jax.numpy as jnp
from jax import lax
from jax.experimental import pallas as pl
from jax.experimental.pallas import tpu as pltpu

F32 = jnp.float32
BF16 = jnp.bfloat16

D_MODEL = 1024
D_RWKV = 512
RWKV_HEAD = 64
RWKV_HEADS = 8
D_HGRN = 512
HGRN_HEAD = 128
HGRN_HEADS = 4
DECAY_LORA = 64
AAA_LORA = 64
GATE_LORA = 128
N_GROUPS = 4
EXPERTS_PER_GROUP = 8
N_EXPERTS = 32
D_EXPERT = 512
NORM_EPS = 1e-6
HGRN_NORM_EPS = 1e-5
LNX_EPS = RWKV_HEAD * 1e-5

C_K = 512
C_V = 1024
C_WD = 1536
C_AD = 1664
C_GD = 1792
C_RWKV_END = 1920
D_IN = 4480
HG_Q = 0
HG_F = 512
HG_I = 1536
HG_G = 2048
D_HG_IN = 2560

SUBLANES = 8
LANES = 128
VMEM_LIMIT = 56 * 1024 * 1024

PROJ_ROWS = 256
HALO = 8
SCAN_STEPS = 64
HGRN_CHUNK = 128
HGRN_SUB = 8
LOG2E = 1.4426950408889634
HGRN_TILE = 512
MOE_ROWS = 256
ROUTER_PAD = 128
RELAYOUT_T = 128

Z_R, Z_KK, Z_W, Z_K, Z_B, Z_V = 0, 1, 2, 4, 6, 8
Z_COUNT = 9
SHIFT_R, SHIFT_W, SHIFT_K, SHIFT_B = 0, 1, 3, 2


def _dot(a, b):
    return jnp.dot(a, b, preferred_element_type=F32)


def _dot_nt(a, b):
    return lax.dot_general(a, b, (((1,), (1,)), ((), ())), preferred_element_type=F32)


def _dot_tn(a, b):
    return lax.dot_general(a, b, (((0,), (0,)), ((), ())), preferred_element_type=F32)


def _split2(a):
    hi = a.astype(BF16)
    lo = (a - hi.astype(F32)).astype(BF16)
    return hi, lo


def _split3(a):
    hi = a.astype(BF16)
    r1 = a - hi.astype(F32)
    mid = r1.astype(BF16)
    lo = (r1 - mid.astype(F32)).astype(BF16)
    return hi, mid, lo


def _dot_lhs2(a, b_bf16):
    hi, lo = _split2(a)
    return _dot(hi, b_bf16) + _dot(lo, b_bf16)


def _sigmoid(x):
    return 1.0 / (1.0 + jnp.exp(-x))


def _softplus(x):
    return jnp.maximum(x, 0.0) + jnp.log(1.0 + jnp.exp(-jnp.abs(x)))


def _lanes(c, n):
    if n == LANES:
        return c
    if n < LANES:
        return c[:, :n]
    return jnp.concatenate([c] * (n // LANES), axis=1)


def _dot_rhs2(a_bf16, b):
    hi, lo = _split2(b)
    return _dot(a_bf16, hi) + _dot(a_bf16, lo)


def _proj_kernel(x_ref, xp_ref, xn_ref, n1_ref, wr_ref, wh_ref, mu_ref, w0_ref, w2_ref, a0_ref, a2_ref,
                 g2_ref, kk_ref, ka_ref, rk_ref, lb_ref, hd_ref,
                 z_o, g_o, bv_o, q_o, lf_o, kf_o, iv_o, sg_o):
    i = pl.program_id(1)
    n_i = pl.num_programs(1)
    rows = x_ref.shape[1]

    def norm(xf):
        ms = jnp.mean(xf * xf, axis=-1, keepdims=True)
        return (xf * lax.rsqrt(ms + NORM_EPS) * n1_ref[...]).astype(BF16)

    h_c = norm(x_ref[0])
    h_h = norm(jnp.concatenate([xp_ref[0] * (i > 0).astype(F32), xn_ref[0] * (i < n_i - 1).astype(F32)], axis=0))

    wr = wr_ref[...]
    u_c = _dot_nt(wr, h_c)
    u_h = _dot_nt(wr, h_h)
    lane = lax.broadcasted_iota(jnp.int32, (1, rows), 1)
    u_prev = jnp.where(lane == 0, u_h[:, HALO - 1:HALO], pltpu.roll(u_c, 1, 1))
    u_next = jnp.where(lane == rows - 1, u_h[:, HALO:HALO + 1], pltpu.roll(u_c, rows - 1, 1))
    us = u_c + (0.5 * (u_prev + u_next) - u_c) * _lanes(mu_ref[...], rows)

    def put(j, a):
        z_o[j] = a.reshape(RWKV_HEAD, SUBLANES, rows)

    r = us[0:C_K]
    k = us[C_K:C_V]
    v = us[C_V:C_WD]
    hd = hd_ref[...]
    kk = k * _lanes(kk_ref[...], rows)
    kn = kk / jnp.maximum(jnp.sqrt(_dot_rhs2(hd, kk * kk)), 1e-12)
    gd = us[C_GD:C_RWKV_END]
    g_o[0] = _dot_tn(_sigmoid(gd).astype(BF16), g2_ref[...])
    put(Z_R, r)
    put(Z_KK, kn)
    put(Z_V, v)
    kd_sum = jnp.zeros_like(k)
    for d in range(2):
        wd = us[C_WD + d * DECAY_LORA:C_WD + (d + 1) * DECAY_LORA]
        ad = us[C_AD + d * AAA_LORA:C_AD + (d + 1) * AAA_LORA]
        zw = _lanes(w0_ref[d], rows) + _dot(w2_ref[d], jnp.tanh(wd).astype(BF16))
        w_log = -_softplus(-zw) - 0.5
        put(Z_W + d, jnp.exp(-jnp.exp(w_log)))
        a = _sigmoid(_lanes(a0_ref[d], rows) + _dot(a2_ref[d], ad.astype(BF16)))
        kd = k * (1.0 + (a - 1.0) * _lanes(ka_ref[...], rows))
        put(Z_K + d, kd)
        put(Z_B + d, kn * a)
        kd_sum = kd_sum + kd
    bonus = _dot_rhs2(hd, r * kd_sum * _lanes(rk_ref[...], rows))
    bv_o[0] = (bonus * v).T

    uh = _dot(h_c, wh_ref[...])
    uq = uh[:, HG_Q:HG_F]
    q_o[0] = uq * _sigmoid(uq)
    iv_o[0] = uh[:, HG_I:HG_G]
    ug = uh[:, HG_G:D_HG_IN]
    sg_o[0] = ug * _sigmoid(ug)
    for d in range(2):
        fr = uh[:, HG_F + d * D_HGRN:HG_F + (d + 1) * D_HGRN]
        lb = lb_ref[d:d + 1, :]
        f = lb + (1.0 - lb) * _sigmoid(fr)
        lf_o[d, 0] = jnp.log(f)
        kf_o[d, 0] = 1.0 - f


def _proj_call(x, p):
    B, T, D = x.shape
    rows = min(PROJ_ROWS, T)
    n_i = T // rows
    rb = rows // HALO

    def full(a):
        nd = a.ndim
        return pl.BlockSpec(a.shape, lambda b, i, _n=nd: (0,) * _n)

    tile = pl.BlockSpec((1, rows, D), lambda b, i: (b, i, 0))
    prev = pl.BlockSpec((1, HALO, D), lambda b, i: (b, jnp.maximum(i * rb - 1, 0), 0))
    nxt = pl.BlockSpec((1, HALO, D), lambda b, i: (b, jnp.minimum((i + 1) * rb, T // HALO - 1), 0))
    consts = [p["norm1_w"], p["w_r"], p["w_h"], p["mu"], p["w0"], p["w2"], p["a0"], p["a2"], p["g2"],
              p["k_k"], p["k_a"], p["r_k"], p["lb"], p["hd64"]]
    one = jax.ShapeDtypeStruct((B, T, D_RWKV), F32)
    two = jax.ShapeDtypeStruct((2, B, T, D_RWKV), F32)
    one_spec = pl.BlockSpec((1, rows, D_RWKV), lambda b, i: (b, i, 0))
    two_spec = pl.BlockSpec((2, 1, rows, D_RWKV), lambda b, i: (0, b, i, 0))
    z = jax.ShapeDtypeStruct((Z_COUNT, RWKV_HEAD, B * RWKV_HEADS, T), F32)
    z_spec = pl.BlockSpec((Z_COUNT, RWKV_HEAD, RWKV_HEADS, rows), lambda b, i: (0, 0, b, i))
    kinds = [z, one, one, one, two, two, one, one]
    return pl.pallas_call(
        _proj_kernel,
        out_shape=kinds,
        grid=(B, n_i),
        in_specs=[tile, prev, nxt] + [full(a) for a in consts],
        out_specs=[z_spec if s is z else (one_spec if s is one else two_spec) for s in kinds],
        compiler_params=pltpu.CompilerParams(
            dimension_semantics=("parallel", "arbitrary"), vmem_limit_bytes=VMEM_LIMIT),
        name="proj",
    )(x, x, x, *consts)


def _rwkv_kernel(r_ref, kk_ref, v_ref, w_ref, k_ref, b_ref, y_ref, s_ref, *, nv, steps):
    d = pl.program_id(0)
    i = pl.program_id(1)
    K = RWKV_HEAD
    n_par = max(1, SUBLANES // nv)

    @pl.when(i == 0)
    def _():
        s_ref[...] = jnp.zeros_like(s_ref)

    def row(ref, tt, k):
        return jnp.broadcast_to(ref[tt[0], k, pl.ds(tt[1], 1), :], (SUBLANES, LANES))

    def tree_sum(parts):
        parts = [x for x in parts if x is not None]
        while len(parts) > 1:
            parts = [parts[j] + parts[j + 1] if j + 1 < len(parts) else parts[j]
                     for j in range(0, len(parts), 2)]
        return parts[0]

    def step(j, carry):
        t = jnp.where(d == 0, j, steps - 1 - j)
        tt = (t // SUBLANES, t % SUBLANES)
        acc = [[None] * n_par for _ in range(nv)]
        for k in range(K):
            kk_row = row(kk_ref, tt, (k + SHIFT_W) % K)
            for vg in range(nv):
                term = s_ref[vg * K + k] * kk_row
                slot = k % n_par
                acc[vg][slot] = term if acc[vg][slot] is None else acc[vg][slot] + term
        sa = [-tree_sum(acc[vg]) for vg in range(nv)]
        val = [v_ref[t, pl.ds(vg * SUBLANES, SUBLANES), :] for vg in range(nv)]
        yacc = [[None] * n_par for _ in range(nv)]
        for k in range(K):
            w_row = row(w_ref, tt, (k + SHIFT_W) % K)
            b_row = row(b_ref, tt, (k + SHIFT_B) % K)
            k_row = row(k_ref, tt, (k + SHIFT_K) % K)
            r_row = row(r_ref, tt, (k + SHIFT_R) % K)
            for vg in range(nv):
                s_new = s_ref[vg * K + k] * w_row + sa[vg] * b_row + val[vg] * k_row
                s_ref[vg * K + k] = s_new
                term = s_new * r_row
                slot = k % n_par
                yacc[vg][slot] = term if yacc[vg][slot] is None else yacc[vg][slot] + term
        for vg in range(nv):
            y_ref[t, pl.ds(vg * SUBLANES, SUBLANES), :] = tree_sum(yacc[vg])
        return carry

    lax.fori_loop(0, steps, step, 0)


def _rwkv_call(a, v, nv):
    T = a.shape[1] * SUBLANES
    steps = min(SCAN_STEPS, T)
    n_t = T // steps
    nvp = nv * SUBLANES

    def tblk(d, i):
        return jnp.where(d == 0, i, n_t - 1 - i)

    kblock = (None, steps // SUBLANES, RWKV_HEAD, SUBLANES, LANES)

    def shared(j):
        return pl.BlockSpec(kblock, lambda d, i: (j, tblk(d, i), 0, 0, 0))

    def perdir(j):
        return pl.BlockSpec(kblock, lambda d, i: (j + d, tblk(d, i), 0, 0, 0))

    vspec = pl.BlockSpec((steps, nvp, LANES), lambda d, i: (tblk(d, i), 0, 0))
    yspec = pl.BlockSpec((None, steps, nvp, LANES), lambda d, i: (d, tblk(d, i), 0, 0))
    return pl.pallas_call(
        functools.partial(_rwkv_kernel, nv=nv, steps=steps),
        out_shape=jax.ShapeDtypeStruct((2, T, nvp, LANES), F32),
        grid=(2, n_t),
        in_specs=[shared(Z_R), shared(Z_KK), vspec, perdir(Z_W), perdir(Z_K), perdir(Z_B)],
        out_specs=yspec,
        scratch_shapes=[pltpu.VMEM((nv * RWKV_HEAD, SUBLANES, LANES), F32)],
        compiler_params=pltpu.CompilerParams(
            dimension_semantics=("parallel", "arbitrary"), vmem_limit_bytes=VMEM_LIMIT),
        name="rwkv",
    )(a, a, v, a, a, a)


def _plane_shift(j):
    return jnp.where(j < Z_KK, SHIFT_R, jnp.where(j < Z_K, SHIFT_W, jnp.where(j < Z_B, SHIFT_K, SHIFT_B)))


def _to_chains_k_kernel(z_ref, o_ref, *, vq):
    tl = z_ref.shape[2]
    shift = _plane_shift(pl.program_id(0))
    for p in range(RWKV_HEAD):
        a = z_ref[p]
        a = (jnp.concatenate([a] * vq, axis=0) if vq > 1 else a).T
        o_ref[:, (p + shift) % RWKV_HEAD] = a.reshape(tl // SUBLANES, SUBLANES, LANES)


def _to_chains_v_kernel(z_ref, o_ref, *, vq, nvp):
    for p in range(nvp):
        o_ref[:, p, :] = jnp.concatenate([z_ref[q * nvp + p] for q in range(vq)], axis=0).T


def _to_chains_call(z, vq, nvp):
    n_arr, planes, C, T = z.shape
    tl = min(RELAYOUT_T, T)
    a = pl.pallas_call(
        functools.partial(_to_chains_k_kernel, vq=vq),
        out_shape=jax.ShapeDtypeStruct((Z_V, T // SUBLANES, planes, SUBLANES, LANES), F32),
        grid=(Z_V, T // tl),
        in_specs=[pl.BlockSpec((None, planes, C, tl), lambda j, t: (j, 0, 0, t))],
        out_specs=pl.BlockSpec((None, tl // SUBLANES, planes, SUBLANES, LANES), lambda j, t: (j, t, 0, 0, 0)),
        compiler_params=pltpu.CompilerParams(
            dimension_semantics=("parallel", "parallel"), vmem_limit_bytes=VMEM_LIMIT),
        name="to_chains_k",
    )(z)
    v = pl.pallas_call(
        functools.partial(_to_chains_v_kernel, vq=vq, nvp=nvp),
        out_shape=jax.ShapeDtypeStruct((T, nvp, LANES), F32),
        grid=(T // tl,),
        in_specs=[pl.BlockSpec((None, planes, C, tl), lambda t: (Z_V, 0, 0, t))],
        out_specs=pl.BlockSpec((tl, nvp, LANES), lambda t: (t, 0, 0)),
        compiler_params=pltpu.CompilerParams(
            dimension_semantics=("parallel",), vmem_limit_bytes=VMEM_LIMIT),
        name="to_chains_v",
    )(z)
    return a, v


def _from_chains_kernel(y_ref, o_ref, zs, *, vq, nvp):
    B = o_ref.shape[0]
    C = B * RWKV_HEADS
    tl = o_ref.shape[1]
    for p in range(nvp):
        zs[p] = (y_ref[0, :, p, :] + y_ref[1, :, p, :]).T
    for b in range(B):
        rows = [zs[:, pl.ds(q * C + b * RWKV_HEADS, RWKV_HEADS), :] for q in range(vq)]
        slab = jnp.concatenate(rows, axis=0) if vq > 1 else rows[0]
        o_ref[b] = slab.reshape(D_RWKV, tl).T


def _from_chains_call(y, B, vq, nvp):
    T = y.shape[1]
    tl = min(RELAYOUT_T, T)
    return pl.pallas_call(
        functools.partial(_from_chains_kernel, vq=vq, nvp=nvp),
        out_shape=jax.ShapeDtypeStruct((B, T, D_RWKV), F32),
        grid=(T // tl,),
        in_specs=[pl.BlockSpec((2, tl, nvp, LANES), lambda t: (0, t, 0, 0))],
        out_specs=pl.BlockSpec((B, tl, D_RWKV), lambda t: (0, t, 0)),
        scratch_shapes=[pltpu.VMEM((nvp, LANES, tl), F32)],
        compiler_params=pltpu.CompilerParams(
            dimension_semantics=("parallel",), vmem_limit_bytes=VMEM_LIMIT),
        name="from_chains",
    )(y)


def _hgrn_direction(q, kf, val, lf, st_ref, tri, off_mask, sel, ones, rev, sub):
    C = q.shape[0]
    l_hi, l_mid, l_lo = _split3(lf)
    G = (_dot(tri, l_hi) + _dot(tri, l_mid) + _dot(tri, l_lo)) * LOG2E
    last = 0 if rev else C - 1
    g_tot = G[last:last + 1, :]
    outs = []
    for h in range(HGRN_HEADS):
        sl = slice(h * HGRN_HEAD, (h + 1) * HGRN_HEAD)
        Gh, qh, kh, vh = G[:, sl], q[:, sl], kf[:, sl], val[:, sl]
        st = st_ref[h]
        o = _dot_nt((qh * jnp.exp2(Gh)).astype(BF16), st.astype(BF16))
        a_off = None
        for lv, L in enumerate(_hgrn_levels(C, sub)):
            mid = L // 2 if rev else L // 2 - 1
            refs = [jnp.broadcast_to(Gh[m * L + mid:m * L + mid + 1, :], (L, HGRN_HEAD)) for m in range(C // L)]
            ref = jnp.concatenate(refs, axis=0) if len(refs) > 1 else refs[0]
            ql = qh * jnp.exp2(jnp.minimum(Gh - ref, 0.0))
            kl = kh * jnp.exp2(jnp.minimum(ref - Gh, 0.0))
            al = _dot_nt(ql.astype(BF16), kl.astype(BF16)) * off_mask[lv]
            a_off = al if a_off is None else a_off + al
        if a_off is not None:
            o = o + _dot(a_off.astype(BF16), vh.astype(BF16))
        parts = []
        for t in range(C):
            b0 = (t // sub) * sub
            rel = jnp.exp2(jnp.minimum(Gh[t:t + 1, :] - Gh[b0:b0 + sub, :], 0.0))
            parts.append(rel * kh[b0:b0 + sub, :] * qh[t:t + 1, :])
        ab = _dot(jnp.concatenate(parts, axis=0).astype(BF16), ones)
        wv = jnp.concatenate(
            [ab[t * sub:(t + 1) * sub] * vh[(t // sub) * sub:(t // sub + 1) * sub] for t in range(C)], axis=0)
        o = o + _dot(sel, wv.astype(BF16))
        outs.append(o)
        kdec = kh * jnp.exp2(g_tot[:, sl] - Gh)
        st_ref[h] = st * jnp.exp2(g_tot[:, sl]) + _dot_tn(vh.astype(BF16), kdec.astype(BF16))
    return jnp.concatenate(outs, axis=-1)


def _hgrn_kernel(qf_ref, vf_ref, lff_ref, kff_ref, qb_ref, vb_ref, lfb_ref, kfb_ref,
                 trif_ref, trib_ref, offf_ref, offb_ref, self_ref, selb_ref, ones_ref,
                 of_ref, ob_ref, stf_ref, stb_ref, *, chunk, sub):
    i = pl.program_id(1)
    n_c = qf_ref.shape[1] // chunk

    @pl.when(i == 0)
    def _():
        stf_ref[...] = jnp.zeros_like(stf_ref)
        stb_ref[...] = jnp.zeros_like(stb_ref)

    def body(c, carry):
        cf = pl.multiple_of(c * chunk, chunk)
        cb = pl.multiple_of((n_c - 1 - c) * chunk, chunk)
        of_ref[0, pl.ds(cf, chunk), :] = _hgrn_direction(
            qf_ref[0, pl.ds(cf, chunk), :], kff_ref[0, pl.ds(cf, chunk), :], vf_ref[0, pl.ds(cf, chunk), :],
            lff_ref[0, pl.ds(cf, chunk), :], stf_ref, trif_ref[...], offf_ref[...], self_ref[...], ones_ref[...],
            rev=False, sub=sub)
        ob_ref[0, pl.ds(cb, chunk), :] = _hgrn_direction(
            qb_ref[0, pl.ds(cb, chunk), :], kfb_ref[0, pl.ds(cb, chunk), :], vb_ref[0, pl.ds(cb, chunk), :],
            lfb_ref[0, pl.ds(cb, chunk), :], stb_ref, trib_ref[...], offb_ref[...], selb_ref[...], ones_ref[...],
            rev=True, sub=sub)
        return carry

    lax.fori_loop(0, n_c, body, 0)


def _hgrn_levels(chunk, sub):
    levels = []
    L = 2 * sub
    while L <= chunk:
        levels.append(L)
        L *= 2
    return levels


def _hgrn_consts(chunk, sub):
    t = jnp.arange(chunk)
    tri_f = (t[None, :] <= t[:, None])
    off_f = jnp.stack([(t[:, None] // L == t[None, :] // L) & (t[:, None] % L >= L // 2) & (t[None, :] % L < L // 2)
                       for L in _hgrn_levels(chunk, sub)] or [jnp.zeros((chunk, chunk), bool)])
    col = jnp.arange(chunk * sub)
    same_t = (col // sub)[None, :] == t[:, None]
    j = (col % sub)[None, :]
    sel_f = same_t & (j <= (t % sub)[:, None])
    sel_b = same_t & (j >= (t % sub)[:, None])
    return [tri_f.astype(BF16), tri_f.T.astype(BF16), off_f.astype(F32), jnp.swapaxes(off_f, 1, 2).astype(F32),
            sel_f.astype(BF16), sel_b.astype(BF16), jnp.ones((HGRN_HEAD, HGRN_HEAD), BF16)]


def _hgrn_call(q, iv, lf, kf):
    B, T, _ = q.shape
    tile = min(HGRN_TILE, T)
    chunk = min(HGRN_CHUNK, tile)
    sub = min(HGRN_SUB, chunk)
    n_i = T // tile
    consts = _hgrn_consts(chunk, sub)
    fwd = pl.BlockSpec((1, tile, D_HGRN), lambda b, i: (b, i, 0))
    bwd = pl.BlockSpec((1, tile, D_HGRN), lambda b, i: (b, n_i - 1 - i, 0))
    fwd2 = pl.BlockSpec((None, 1, tile, D_HGRN), lambda b, i: (0, b, i, 0))
    bwd2 = pl.BlockSpec((None, 1, tile, D_HGRN), lambda b, i: (1, b, n_i - 1 - i, 0))
    full = [pl.BlockSpec(a.shape, lambda b, i, _n=a.ndim: (0,) * _n) for a in consts]
    out = jax.ShapeDtypeStruct((B, T, D_HGRN), F32)
    state = pltpu.VMEM((HGRN_HEADS, HGRN_HEAD, HGRN_HEAD), F32)
    return pl.pallas_call(
        functools.partial(_hgrn_kernel, chunk=chunk, sub=sub),
        out_shape=[out, out],
        grid=(B, n_i),
        in_specs=[fwd, fwd, fwd2, fwd2, bwd, bwd, bwd2, bwd2] + full,
        out_specs=[fwd, bwd],
        scratch_shapes=[state, state],
        compiler_params=pltpu.CompilerParams(
            dimension_semantics=("parallel", "arbitrary"), vmem_limit_bytes=VMEM_LIMIT),
        name="hgrn",
    )(q, iv, lf, kf, q, iv, lf, kf, *consts)


def _mix_kernel(x_ref, ya_ref, g_ref, bv_ref, of_ref, ob_ref, sg_ref, lnw_ref, lnb_ref, gnw_ref,
                hd64_ref, hd128_ref, wo_ref, n2_ref, rwh_ref, rwl_ref, rb_ref,
                x1_o, h2_o, lg_o):
    hd64 = hd64_ref[...]
    y = ya_ref[0]
    inv = 1.0 / RWKV_HEAD
    mean = _dot_lhs2(y, hd64) * inv
    yc = y - mean
    var = _dot_lhs2(yc * yc, hd64) * inv
    ya = (yc * lax.rsqrt(var + LNX_EPS)) * lnw_ref[...] + lnb_ref[...]
    ya = (ya + bv_ref[0]) * g_ref[0]
    o = of_ref[0] + ob_ref[0]
    ms = _dot_lhs2(o * o, hd128_ref[...]) * (1.0 / HGRN_HEAD)
    yb = o * lax.rsqrt(ms + HGRN_NORM_EPS) * gnw_ref[...] * sg_ref[0]
    mixed = jnp.concatenate([ya, yb], axis=-1).astype(BF16)
    x1 = x_ref[0] + _dot(mixed, wo_ref[...])
    x1_o[0] = x1
    ms2 = jnp.mean(x1 * x1, axis=-1, keepdims=True)
    h2 = x1 * lax.rsqrt(ms2 + NORM_EPS) * n2_ref[...]
    h2_o[0] = h2
    h_hi, h_lo = _split2(h2)
    w_hi = rwh_ref[...]
    lg_o[0] = (_dot_nt(w_hi, h_hi) + _dot_nt(w_hi, h_lo) + _dot_nt(rwl_ref[...], h_hi)
               + _lanes(rb_ref[...], h2.shape[0]))


def _mix_call(x, ya, g, bv, of, ob, sg, p):
    B, T, D = x.shape
    rows = min(PROJ_ROWS, T)

    def full(a):
        nd = a.ndim
        return pl.BlockSpec(a.shape, lambda b, i, _n=nd: (0,) * _n)

    def tile(w):
        return pl.BlockSpec((1, rows, w), lambda b, i: (b, i, 0))

    consts = [p["lnx_w"], p["lnx_b"], p["gnorm_w"], p["hd64"], p["hd128"], p["w_out"], p["norm2_w"],
              p["rw_hi"], p["rw_lo"], p["rb"]]
    return pl.pallas_call(
        _mix_kernel,
        out_shape=[jax.ShapeDtypeStruct((B, T, D), F32), jax.ShapeDtypeStruct((B, T, D), F32),
                   jax.ShapeDtypeStruct((B, ROUTER_PAD, T), F32)],
        grid=(B, T // rows),
        in_specs=[tile(D)] + [tile(D_RWKV)] * 6 + [full(a) for a in consts],
        out_specs=[tile(D), tile(D), pl.BlockSpec((1, ROUTER_PAD, rows), lambda b, i: (b, 0, i))],
        compiler_params=pltpu.CompilerParams(
            dimension_semantics=("parallel", "parallel"), vmem_limit_bytes=VMEM_LIMIT),
        name="mix",
    )(x, ya, g, bv, of, ob, sg, *consts)


def _moe_kernel(be_ref, nu_ref, tok0_ref, tokn_ref, dst_ref, gt_ref, h_hbm, wg_ref, wu_ref, wd_ref, y_hbm,
                xbuf, ybuf, gsem, ssem):
    del be_ref
    i = pl.program_id(0)
    nb_used = nu_ref[0]
    slot = lax.rem(i, 2)

    def gather_copy(tok, r, s):
        return pltpu.make_async_copy(h_hbm.at[pl.ds(tok, 1)], xbuf.at[s, pl.ds(r, 1)], gsem.at[s])

    def scatter_copy(dst, r, s):
        return pltpu.make_async_copy(ybuf.at[s, pl.ds(r, 1)], y_hbm.at[pl.ds(dst, 1)], ssem.at[s])

    def start_gather(tok_ref, s):
        def body(r, c):
            gather_copy(tok_ref[0, 0, r], r, s).start()
            return c
        lax.fori_loop(0, MOE_ROWS, body, 0, unroll=8)

    def start_scatter(s):
        def body(r, c):
            scatter_copy(dst_ref[0, 0, r], r, s).start()
            return c
        lax.fori_loop(0, MOE_ROWS, body, 0, unroll=8)

    def wait_gather(s):
        pltpu.make_async_copy(h_hbm.at[pl.ds(0, MOE_ROWS)], xbuf.at[s], gsem.at[s]).wait()

    def wait_scatter(s):
        pltpu.make_async_copy(ybuf.at[s], y_hbm.at[pl.ds(0, MOE_ROWS)], ssem.at[s]).wait()

    @pl.when(i == 0)
    def _():
        plane = y_hbm.shape[0] // 2
        ybuf[...] = jnp.zeros_like(ybuf)
        for s in range(2):
            spare = pltpu.make_async_copy(
                ybuf.at[s], y_hbm.at[pl.ds((s + 1) * plane - MOE_ROWS, MOE_ROWS)], ssem.at[s])
            spare.start()
            spare.wait()

    @pl.when(jnp.logical_and(i == 0, nb_used > 0))
    def _():
        start_gather(tok0_ref, 0)

    @pl.when(i + 1 < nb_used)
    def _():
        start_gather(tokn_ref, 1 - slot)

    @pl.when(i < nb_used)
    def _():
        wait_gather(slot)

        @pl.when(i >= 2)
        def _():
            wait_scatter(slot)

        x = xbuf[slot].astype(BF16)
        a = _dot(x, wg_ref[0])
        u = _dot(x, wu_ref[0])
        hid = (a * _sigmoid(a)) * u
        ybuf[slot] = _dot(hid.astype(BF16), wd_ref[0]) * gt_ref[...]
        start_scatter(slot)

        @pl.when(i == nb_used - 1)
        def _():
            wait_scatter(slot)

            @pl.when(i >= 1)
            def _():
                wait_scatter(1 - slot)


def _moe_call(block_e, nb_used, src_tok, dst_row, gate_slot, h2, p):
    N, D = h2.shape
    P = src_tok.shape[0]
    n_blocks = P // MOE_ROWS
    smem_blk = lambda f: pl.BlockSpec((1, 1, MOE_ROWS), f, memory_space=pltpu.SMEM)
    grid_spec = pltpu.PrefetchScalarGridSpec(
        num_scalar_prefetch=2,
        grid=(n_blocks,),
        in_specs=[
            smem_blk(lambda i, be, nu: (0, 0, 0)),
            smem_blk(lambda i, be, nu: (jnp.minimum(i + 1, n_blocks - 1), 0, 0)),
            smem_blk(lambda i, be, nu: (i, 0, 0)),
            pl.BlockSpec((MOE_ROWS, 1), lambda i, be, nu: (i, 0)),
            pl.BlockSpec(memory_space=pl.ANY),
            pl.BlockSpec((1, D, D_EXPERT), lambda i, be, nu: (be[i], 0, 0)),
            pl.BlockSpec((1, D, D_EXPERT), lambda i, be, nu: (be[i], 0, 0)),
            pl.BlockSpec((1, D_EXPERT, D), lambda i, be, nu: (be[i], 0, 0)),
        ],
        out_specs=pl.BlockSpec(memory_space=pl.ANY),
        scratch_shapes=[pltpu.VMEM((2, MOE_ROWS, D), F32), pltpu.VMEM((2, MOE_ROWS, D), F32),
                        pltpu.SemaphoreType.DMA((2,)), pltpu.SemaphoreType.DMA((2,))],
    )
    tok3 = src_tok.reshape(n_blocks, 1, MOE_ROWS)
    return pl.pallas_call(
        _moe_kernel,
        out_shape=jax.ShapeDtypeStruct((2 * (N + MOE_ROWS), D), F32),
        grid_spec=grid_spec,
        compiler_params=pltpu.CompilerParams(
            dimension_semantics=("arbitrary",), vmem_limit_bytes=VMEM_LIMIT),
        name="moe",
    )(block_e, nb_used, tok3, tok3, dst_row.reshape(n_blocks, 1, MOE_ROWS), gate_slot, h2,
      p["moe_wg"], p["moe_wu"], p["moe_wd"])


def _route_and_moe(h2, logits, p):
    N, D = h2.shape
    glog = logits[:N_GROUPS]
    elog = logits[N_GROUPS:N_GROUPS + N_EXPERTS].reshape(N_GROUPS, EXPERTS_PER_GROUP, N)
    grp = jnp.argmax(glog, axis=0)
    p_grp = 1.0 / jnp.sum(jnp.exp(glog - jnp.max(glog, axis=0, keepdims=True)), axis=0)
    gsel = jnp.arange(N_GROUPS)[:, None, None] == grp[None, None, :]
    elog_g = jnp.sum(jnp.where(gsel, elog, 0.0), axis=0)
    i1 = jnp.argmax(elog_g, axis=0)
    v1 = jnp.max(elog_g, axis=0)
    rest = jnp.where(jnp.arange(EXPERTS_PER_GROUP)[:, None] == i1[None, :], -jnp.inf, elog_g)
    i2 = jnp.argmax(rest, axis=0)
    e21 = jnp.exp(jnp.max(rest, axis=0) - v1)
    gate = jnp.stack([p_grp / (1.0 + e21), p_grp * e21 / (1.0 + e21)], axis=1).reshape(2 * N)
    eid = (grp[:, None] * EXPERTS_PER_GROUP + jnp.stack([i1, i2], axis=1)).reshape(2 * N).astype(jnp.int32)

    M = 2 * N
    order = jnp.argsort(eid, stable=True).astype(jnp.int32)
    experts = jnp.arange(N_EXPERTS, dtype=jnp.int32)
    e_sorted = eid[order]
    starts = jnp.searchsorted(e_sorted, experts, side="left").astype(jnp.int32)
    counts = jnp.searchsorted(e_sorted, experts, side="right").astype(jnp.int32) - starts
    padded = (counts + MOE_ROWS - 1) // MOE_ROWS * MOE_ROWS
    pad_ends = jnp.cumsum(padded)
    pad_starts = pad_ends - padded
    n_blocks = (M + N_EXPERTS * (MOE_ROWS - 1) + MOE_ROWS - 1) // MOE_ROWS
    P = n_blocks * MOE_ROWS
    block_e = jnp.minimum(
        jnp.sum(pad_ends[None, :] <= (jnp.arange(n_blocks, dtype=jnp.int32) * MOE_ROWS)[:, None], axis=1),
        N_EXPERTS - 1).astype(jnp.int32)
    slot = jnp.arange(P, dtype=jnp.int32)
    off = (slot.reshape(n_blocks, MOE_ROWS) - pad_starts[block_e][:, None])
    valid = ((off >= 0) & (off < counts[block_e][:, None])).reshape(P)
    src = jnp.clip(starts[block_e][:, None] + off, 0, M - 1).reshape(P)
    assign = order[src]
    gate_slot = jnp.where(valid, gate[assign], 0.0)[:, None]
    plane = N + MOE_ROWS
    src_tok = jnp.where(valid, assign // 2, 0)
    spare = ((slot // MOE_ROWS) % 2) * plane + N + slot % MOE_ROWS
    dst_row = jnp.where(valid, (assign % 2) * plane + assign // 2, spare)
    nb_used = (pad_ends[-1] // MOE_ROWS).astype(jnp.int32).reshape(1)
    return _moe_call(block_e, nb_used, src_tok, dst_row, gate_slot, h2, p)


def _final_kernel(x_ref, m0_ref, m1_ref, w_ref, o_ref):
    x = x_ref[...] + m0_ref[...] + m1_ref[...]
    ms = jnp.mean(x * x, axis=-1, keepdims=True)
    o_ref[...] = x * lax.rsqrt(ms + NORM_EPS) * w_ref[...]


def _final_call(x1, moe2, w):
    N, D = x1.shape
    rows = min(PROJ_ROWS * 2, N)
    tile = pl.BlockSpec((rows, D), lambda i: (i, 0))
    moe = moe2.reshape(2, moe2.shape[0] // 2, D)
    return pl.pallas_call(
        _final_kernel,
        out_shape=jax.ShapeDtypeStruct((N, D), F32),
        grid=(N // rows,),
        in_specs=[tile, pl.BlockSpec((None, rows, D), lambda i: (0, i, 0)),
                  pl.BlockSpec((None, rows, D), lambda i: (1, i, 0)), pl.BlockSpec((1, D), lambda i: (0, 0))],
        out_specs=tile,
        compiler_params=pltpu.CompilerParams(
            dimension_semantics=("parallel",), vmem_limit_bytes=VMEM_LIMIT),
        name="final",
    )(x1, moe, moe, w)


def _block_diag_ones(width, head):
    idx = jnp.arange(width) // head
    return (idx[:, None] == idx[None, :]).astype(BF16)


def _prepare(norm1_w, w_in, mu_shift, rwkv_w0, rwkv_w2, rwkv_a0, rwkv_a2, rwkv_g2, rwkv_k_k, rwkv_k_a,
             rwkv_r_k, rwkv_lnx_w, rwkv_lnx_b, hgrn_lb, hgrn_gnorm_w, w_out, norm2_w, router_group_w,
             router_group_b, router_expert_w, router_expert_b, moe_w_gate, moe_w_up, moe_w_down, final_norm_w):
    row = lambda a: a.reshape(1, -1).astype(F32)
    lb_all = jnp.cumsum(jax.nn.softmax(hgrn_lb.astype(F32), axis=1), axis=1)
    rw = jnp.concatenate([router_group_w[0], router_expert_w[0]], axis=1).T
    rw = jnp.pad(rw, ((0, ROUTER_PAD - rw.shape[0]), (0, 0)))
    rw_hi = rw.astype(BF16)
    rw_lo = (rw - rw_hi.astype(F32)).astype(BF16)
    rb = jnp.pad(jnp.concatenate([router_group_b[0], router_expert_b[0]]), (0, ROUTER_PAD - N_GROUPS - N_EXPERTS))
    perm = jnp.arange(D_RWKV).reshape(RWKV_HEADS, RWKV_HEAD).T.reshape(-1)
    cols = jnp.concatenate([perm, C_K + perm, C_V + perm, jnp.arange(C_WD, C_RWKV_END)])
    col = lambda a: jnp.broadcast_to(a.astype(F32)[..., None], a.shape + (LANES,))
    head = jnp.arange(D_RWKV) % RWKV_HEADS
    return {
        "norm1_w": row(norm1_w[0]),
        "w_r": w_in[0][:, :C_RWKV_END].T[cols].astype(BF16),
        "w_h": w_in[0][:, C_RWKV_END:].astype(BF16),
        "mu": col(mu_shift[0][cols]),
        "w0": col(rwkv_w0[0][:, perm]), "w2": jnp.swapaxes(rwkv_w2[0], 1, 2)[:, perm].astype(BF16),
        "a0": col(rwkv_a0[0][:, perm]), "a2": jnp.swapaxes(rwkv_a2[0], 1, 2)[:, perm].astype(BF16),
        "g2": rwkv_g2[0][:, perm].astype(BF16),
        "k_k": col(rwkv_k_k[0][perm]), "k_a": col(rwkv_k_a[0][perm]),
        "r_k": col(rwkv_r_k[0].reshape(-1)[perm]),
        "lnx_w": row(rwkv_lnx_w[0][perm]), "lnx_b": row(rwkv_lnx_b[0][perm]),
        "lb": lb_all[:, 0],
        "gnorm_w": row(jnp.tile(hgrn_gnorm_w[0], HGRN_HEADS)),
        "hd64": (head[:, None] == head[None, :]).astype(BF16),
        "hd128": _block_diag_ones(D_HGRN, HGRN_HEAD),
        "w_out": jnp.concatenate([w_out[0][:D_RWKV][perm], w_out[0][D_RWKV:]]).astype(BF16),
        "norm2_w": row(norm2_w[0]),
        "rw_hi": rw_hi, "rw_lo": rw_lo, "rb": col(rb),
        "moe_wg": moe_w_gate[0].astype(BF16), "moe_wu": moe_w_up[0].astype(BF16),
        "moe_wd": moe_w_down[0].astype(BF16),
        "final_w": row(final_norm_w),
    }


def _forward(x, p):
    B, T, D = x.shape
    z, g, bv, q, lf, kf, iv, sg = _proj_call(x, p)

    chains = B * RWKV_HEADS
    vq = LANES // chains
    nv = RWKV_HEAD // (vq * SUBLANES)
    a, v = _to_chains_call(z, vq, nv * SUBLANES)
    ya = _from_chains_call(_rwkv_call(a, v, nv), B, vq, nv * SUBLANES)

    of, ob = _hgrn_call(q, iv, lf, kf)
    x1, h2, logits = _mix_call(x, ya, g, bv, of, ob, sg, p)
    N = B * T
    moe = _route_and_moe(h2.reshape(N, D), jnp.swapaxes(logits, 0, 1).reshape(ROUTER_PAD, N), p)
    return _final_call(x1.reshape(N, D), moe, p["final_w"]).reshape(B, T, D)


def kernel(x_prompt, x_sample, norm1_w, w_in, mu_shift, rwkv_w0, rwkv_w2, rwkv_a0, rwkv_a2, rwkv_g2, rwkv_k_k, rwkv_k_a, rwkv_r_k, rwkv_lnx_w, rwkv_lnx_b, hgrn_lb, hgrn_gnorm_w, w_out, norm2_w, router_group_w, router_group_b, router_expert_w, router_expert_b, moe_w_gate, moe_w_up, moe_w_down, final_norm_w):
    p = _prepare(norm1_w, w_in, mu_shift, rwkv_w0, rwkv_w2, rwkv_a0, rwkv_a2, rwkv_g2, rwkv_k_k, rwkv_k_a,
                 rwkv_r_k, rwkv_lnx_w, rwkv_lnx_b, hgrn_lb, hgrn_gnorm_w, w_out, norm2_w, router_group_w,
                 router_group_b, router_expert_w, router_expert_b, moe_w_gate, moe_w_up, moe_w_down,
                 final_norm_w)
    return (_forward(x_prompt, p), _forward(x_sample, p))
```
